```python
import math
import jax, jax.numpy as jnp
from jax import lax
import numpy as np

D_MODEL = 2048
BATCH = 1
SEQ = 8192
DEPTH = 4
DEC_BATCH = 2
DEC_SEQ = 8192
PAST_LEN = 128

N_MIXERS = 2
N_ATTN_LAYERS = (DEPTH + 1) // 2
N_SSD_LAYERS = DEPTH // 2

GRID_W = 64

HEAD_DIM = 128
N_HEADS = D_MODEL // HEAD_DIM
N_KV_HEADS = N_HEADS // 2
GQ = N_HEADS // N_KV_HEADS
D_Q = N_HEADS * HEAD_DIM
D_KV = N_KV_HEADS * HEAD_DIM
ROPE_AXIS_DIM = HEAD_DIM // 2
ROPE_THETA = 10000.0
Q_BLOCK = 128

SSD_EXPAND = 2
D_INNER = SSD_EXPAND * D_MODEL
SSD_HEAD_DIM = 64
SSD_HEADS = D_INNER // SSD_HEAD_DIM
SSD_GROUPS = 8
SSD_HEADS_PER_GROUP = SSD_HEADS // SSD_GROUPS
D_STATE = 128
D_CONV = 5
CHUNK = 128
CONV_DIM = D_INNER + 2 * SSD_GROUPS * D_STATE
D_IN_PROJ = D_INNER + CONV_DIM + 2 * SSD_HEADS

D_FF = 4 * D_MODEL

NORM_EPS = 1e-6

kernel_name = 'hybrid_axial_gqa_bissd_encoder'


def _rmsnorm(x, g):
    x32 = x.astype(jnp.float32)
    y = x32 * lax.rsqrt(jnp.mean(x32 * x32, axis=-1, keepdims=True) + NORM_EPS)
    return (y * g.astype(jnp.float32)).astype(x.dtype)


def _axial_rope_tables(seq_len):
    rows = seq_len // GRID_W
    row = jnp.repeat(jnp.arange(rows, dtype=jnp.float32), GRID_W)
    col = jnp.tile(jnp.arange(GRID_W, dtype=jnp.float32), rows)
    inv_freq = ROPE_THETA ** (-jnp.arange(0, ROPE_AXIS_DIM, 2, dtype=jnp.float32) / ROPE_AXIS_DIM)
    ang = jnp.stack([row, col], axis=-1)[..., None] * inv_freq
    return jnp.cos(ang), jnp.sin(ang)


def _apply_axial_rope(x, cos, sin):
    L = x.shape[1]
    bshape = (1, L) + (1,) * (x.ndim - 3) + cos.shape[1:]
    c = cos.reshape(bshape)
    s = sin.reshape(bshape)
    xr = x.astype(jnp.float32).reshape(x.shape[:-1] + (2, 2, ROPE_AXIS_DIM // 2))
    x1, x2 = xr[..., 0, :], xr[..., 1, :]
    out = jnp.stack([x1 * c - x2 * s, x2 * c + x1 * s], axis=-2)
    return out.reshape(x.shape).astype(x.dtype)


def _attention_mixer(u, w_qkv, q_gain, k_gain, w_o, cos, sin):
    b, L, _ = u.shape
    qkv = u @ w_qkv
    q = qkv[..., :D_Q].reshape(b, L, N_KV_HEADS, GQ, HEAD_DIM)
    k = qkv[..., D_Q:D_Q + D_KV].reshape(b, L, N_KV_HEADS, HEAD_DIM)
    v = qkv[..., D_Q + D_KV:].reshape(b, L, N_KV_HEADS, HEAD_DIM)
    q = _apply_axial_rope(_rmsnorm(q, q_gain), cos, sin)
    k = _apply_axial_rope(_rmsnorm(k, k_gain), cos, sin)
    scale = HEAD_DIM ** -0.5
    nb = L // Q_BLOCK
    qb = q.reshape(b, nb, Q_BLOCK, N_KV_HEADS, GQ, HEAD_DIM).transpose(1, 0, 2, 3, 4, 5)

    def block(qi):
        s = jnp.einsum('bqkgd,bskd->bkgqs', qi, k).astype(jnp.float32) * scale
        p = jax.nn.softmax(s, axis=-1).astype(v.dtype)
        return jnp.einsum('bkgqs,bskd->bqkgd', p, v)

    o = lax.map(block, qb)
    o = o.transpose(1, 0, 2, 3, 4, 5).reshape(b, L, D_Q)
    return o @ w_o


def _depthwise_conv_centred(x, w, bias):
    pad = (D_CONV - 1) // 2
    y = lax.conv_general_dilated(x, w[:, None, :], window_strides=(1,), padding=[(pad, pad)],
                                 dimension_numbers=('NWC', 'WIO', 'NWC'),
                                 feature_group_count=x.shape[-1])
    return y + bias


def _ssd_scan(x, dt, A, Bm, Cm):
    b, L = x.shape[:2]
    c = L // CHUNK
    x = x.reshape(b, c, CHUNK, SSD_GROUPS, SSD_HEADS_PER_GROUP, SSD_HEAD_DIM)
    dt = dt.reshape(b, c, CHUNK, SSD_GROUPS, SSD_HEADS_PER_GROUP)
    Bm = Bm.reshape(b, c, CHUNK, SSD_GROUPS, D_STATE)
    Cm = Cm.reshape(b, c, CHUNK, SSD_GROUPS, D_STATE)
    cs = jnp.cumsum(dt * A, axis=2)
    xdt = x * dt[..., None]
    mask = jnp.tril(jnp.ones((CHUNK, CHUNK), dtype=bool))[None, None, :, :, None, None]
    diff = cs[:, :, :, None] - cs[:, :, None, :]
    decay_ls = jnp.exp(jnp.where(mask, diff, -jnp.inf))
    cb = jnp.einsum('bclgn,bcsgn->bclsg', Cm, Bm)
    y_diag = jnp.einsum('bclsgr,bcsgrp->bclgrp', cb[..., None] * decay_ls, xdt)
    decay_to_end = jnp.exp(cs[:, :, -1:] - cs)
    states = jnp.einsum('bclgn,bclgrp->bcgrpn', Bm, xdt * decay_to_end[..., None])
    chunk_decay = jnp.exp(cs[:, :, -1])

    def step(h, inp):
        s, d = inp
        return d[..., None, None] * h + s, h

    h0 = jnp.zeros((b, SSD_GROUPS, SSD_HEADS_PER_GROUP, SSD_HEAD_DIM, D_STATE), jnp.float32)
    _, prev = lax.scan(step, h0, (states.swapaxes(0, 1), chunk_decay.swapaxes(0, 1)))
    prev = prev.swapaxes(0, 1)
    y_off = jnp.einsum('bclgn,bcgrpn->bclgrp', Cm, prev) * jnp.exp(cs)[..., None]
    return (y_diag + y_off).reshape(b, L, SSD_GROUPS, SSD_HEADS_PER_GROUP, SSD_HEAD_DIM)


def _ssd_mixer(u, w_in, conv_w, conv_b, dt_bias, a_log, d_skip, norm_g, w_out):
    b, L, _ = u.shape
    zxbcdt = u @ w_in
    z = zxbcdt[..., :D_INNER]
    xbc = zxbcdt[..., D_INNER:D_INNER + CONV_DIM]
    dt_raw = zxbcdt[..., D_INNER + CONV_DIM:]
    xbc = jax.nn.silu(_depthwise_conv_centred(xbc, conv_w, conv_b)).astype(jnp.float32)
    xs = xbc[..., :D_INNER].reshape(b, L, SSD_GROUPS, SSD_HEADS_PER_GROUP, SSD_HEAD_DIM)
    Bm = xbc[..., D_INNER:D_INNER + SSD_GROUPS * D_STATE].reshape(b, L, SSD_GROUPS, D_STATE)
    Cm = xbc[..., D_INNER + SSD_GROUPS * D_STATE:].reshape(b, L, SSD_GROUPS, D_STATE)
    dt = jax.nn.softplus(dt_raw.astype(jnp.float32).reshape(b, L, 2, SSD_HEADS)
                         + dt_bias.astype(jnp.float32))
    dt = dt.reshape(b, L, 2, SSD_GROUPS, SSD_HEADS_PER_GROUP)
    A = -jnp.exp(a_log.astype(jnp.float32)).reshape(2, SSD_GROUPS, SSD_HEADS_PER_GROUP)
    y_f = _ssd_scan(xs, dt[:, :, 0], A[0], Bm, Cm)
    flip = lambda t: jnp.flip(t, axis=1)
    y_b = flip(_ssd_scan(flip(xs), flip(dt[:, :, 1]), A[1], flip(Bm), flip(Cm)))
    d = d_skip.astype(jnp.float32).reshape(SSD_GROUPS, SSD_HEADS_PER_GROUP)[..., None]
    y = (y_f + y_b + d * xs).reshape(b, L, D_INNER)
    y = y * jax.nn.silu(z.astype(jnp.float32))
    yg = y.reshape(b, L, SSD_GROUPS, D_INNER // SSD_GROUPS)
    yg = yg * lax.rsqrt(jnp.mean(yg * yg, axis=-1, keepdims=True) + NORM_EPS)
    y = yg.reshape(b, L, D_INNER) * norm_g.astype(jnp.float32)
    return y.astype(u.dtype) @ w_out


def _mlp(u, w_up, w_down):
    h = jax.nn.relu(u @ w_up)
    return (h * h) @ w_down


def _trunk(h, norm_mix, norm_mlp, attn_w_qkv, attn_q_norm, attn_k_norm, attn_w_o,
           ssd_w_in, ssd_conv_w, ssd_conv_b, ssd_dt_bias, ssd_a_log, ssd_d, ssd_norm, ssd_w_out,
           mlp_w_up, mlp_w_down):
    cos, sin = _axial_rope_tables(h.shape[1])
    for i in range(DEPTH):
        j = i // N_MIXERS
        u = _rmsnorm(h, norm_mix[i])
        if i % N_MIXERS == 0:
            mix = _attention_mixer(u, attn_w_qkv[j], attn_q_norm[j], attn_k_norm[j], attn_w_o[j], cos, sin)
        else:
            mix = _ssd_mixer(u, ssd_w_in[j], ssd_conv_w[j], ssd_conv_b[j], ssd_dt_bias[j], ssd_a_log[j],
                             ssd_d[j], ssd_norm[j], ssd_w_out[j])
        h = h + mix.astype(h.dtype)
        h = h + _mlp(_rmsnorm(h, norm_mlp[i]), mlp_w_up[i], mlp_w_down[i]).astype(h.dtype)
    return h


def setup_inputs(seed: int = 0) -> dict:
    key = jax.random.key(seed)
    ks = jax.random.split(key, 24)
    f32 = jnp.float32
    nrm = lambda k, shape, s: jax.random.normal(k, shape, f32) * s
    dt0 = jnp.exp(jax.random.uniform(ks[14], (N_SSD_LAYERS, 2, SSD_HEADS), f32,
                                     math.log(1e-3), math.log(1e-1)))
    return {
        'x_prompt': nrm(ks[0], (BATCH, SEQ, D_MODEL), 1.0),
        'x_sample': nrm(ks[1], (DEC_BATCH, DEC_SEQ, D_MODEL), 1.0),
        'norm_mix': 1.0 + nrm(ks[2], (DEPTH, D_MODEL), 0.02),
        'norm_mlp': 1.0 + nrm(ks[3], (DEPTH, D_MODEL), 0.02),
        'attn_w_qkv': nrm(ks[4], (N_ATTN_LAYERS, D_MODEL, D_Q + 2 * D_KV), D_MODEL ** -0.5),
        'attn_q_norm': 1.0 + nrm(ks[5], (N_ATTN_LAYERS, HEAD_DIM), 0.02),
        'attn_k_norm': 1.0 + nrm(ks[6], (N_ATTN_LAYERS, HEAD_DIM), 0.02),
        'attn_w_o': nrm(ks[7], (N_ATTN_LAYERS, D_Q, D_MODEL), D_Q ** -0.5),
        'ssd_w_in': nrm(ks[8], (N_SSD_LAYERS, D_MODEL, D_IN_PROJ), D_MODEL ** -0.5),
        'ssd_conv_w': nrm(ks[9], (N_SSD_LAYERS, D_CONV, CONV_DIM), D_CONV ** -0.5),
        'ssd_conv_b': nrm(ks[10], (N_SSD_LAYERS, CONV_DIM), 0.02),
        'ssd_dt_bias': dt0 + jnp.log(-jnp.expm1(-dt0)),
        'ssd_a_log': jnp.log(jax.random.uniform(ks[11], (N_SSD_LAYERS, 2, SSD_HEADS), f32, 1.0, 16.0)),
        'ssd_d': 1.0 + nrm(ks[12], (N_SSD_LAYERS, SSD_HEADS), 0.02),
        'ssd_norm': 1.0 + nrm(ks[13], (N_SSD_LAYERS, D_INNER), 0.02),
        'ssd_w_out': nrm(ks[15], (N_SSD_LAYERS, D_INNER, D_MODEL), D_INNER ** -0.5),
        'mlp_w_up': nrm(ks[16], (DEPTH, D_MODEL, D_FF), D_MODEL ** -0.5),
        'mlp_w_down': nrm(ks[17], (DEPTH, D_FF, D_MODEL), D_FF ** -0.5),
    }


def reference(x_prompt, x_sample, norm_mix, norm_mlp, attn_w_qkv, attn_q_norm, attn_k_norm, attn_w_o,
              ssd_w_in, ssd_conv_w, ssd_conv_b, ssd_dt_bias, ssd_a_log, ssd_d, ssd_norm, ssd_w_out,
              mlp_w_up, mlp_w_down):
    y_prompt = _trunk(x_prompt, norm_mix, norm_mlp, attn_w_qkv, attn_q_norm, attn_k_norm, attn_w_o,
                      ssd_w_in, ssd_conv_w, ssd_conv_b, ssd_dt_bias, ssd_a_log, ssd_d, ssd_norm, ssd_w_out,
                      mlp_w_up, mlp_w_down)
    y_sample = _trunk(x_sample, norm_mix, norm_mlp, attn_w_qkv, attn_q_norm, attn_k_norm, attn_w_o,
                      ssd_w_in, ssd_conv_w, ssd_conv_b, ssd_dt_bias, ssd_a_log, ssd_d, ssd_norm, ssd_w_out,
                      mlp_w_up, mlp_w_down)
    return (y_prompt, y_sample)
```

```python
import functools
import math

import jax
import jax.numpy as jnp
from jax import lax
from jax.experimental import pallas as pl
from jax.experimental.pallas import tpu as pltpu

F32 = jnp.float32
BF16 = jnp.bfloat16

D_MODEL = 2048
GRID_W = 64
HEAD_DIM = 128
N_HEADS = D_MODEL // HEAD_DIM
N_KV_HEADS = N_HEADS // 2
GQ = N_HEADS // N_KV_HEADS
D_Q = N_HEADS * HEAD_DIM
D_KV = N_KV_HEADS * HEAD_DIM
D_QKV = D_Q + 2 * D_KV
ROPE_AXIS_DIM = HEAD_DIM // 2
ROPE_THETA = 10000.0
D_INNER = 2 * D_MODEL
SSD_HEAD_DIM = 64
SSD_HEADS = D_INNER // SSD_HEAD_DIM
SSD_GROUPS = 8
HEADS_PER_GROUP = SSD_HEADS // SSD_GROUPS
GROUP_WIDTH = D_INNER // SSD_GROUPS
D_STATE = 128
D_CONV = 5
CONV_PAD = (D_CONV - 1) // 2
CHUNK = 128
CONV_DIM = D_INNER + 2 * SSD_GROUPS * D_STATE
D_ZXBC = D_INNER + CONV_DIM
N_DT = 2 * SSD_HEADS
D_FF = 4 * D_MODEL
NORM_EPS = 1e-6

LANES = 128
BF16_SUBLANES = 16
VMEM_LIMIT = 56 * 1024 * 1024

TM = 512
TN_QKV = 512
TN_IN = 1024
TN_OUT = 1024
TF = 1024
TQ = 512
TK = 512
CONV_ROWS = 512
CONV_COLS = 1024


def _cparams(sem):
    return pltpu.CompilerParams(dimension_semantics=sem, vmem_limit_bytes=VMEM_LIMIT)


def _rms_scale(x):
    return lax.rsqrt(jnp.mean(x * x, axis=-1, keepdims=True) + NORM_EPS)


def _qkv_kernel(h_ref, g_ref, w_ref, gain_ref, rc_ref, ra_ref, rb_ref, o_ref, xn_ref):
    j = pl.program_id(1)
    n_q_tiles = D_Q // TN_QKV
    n_qk_tiles = (D_Q + D_KV) // TN_QKV

    @pl.when(j == 0)
    def _():
        x = h_ref[...]
        xn_ref[...] = (x * _rms_scale(x) * g_ref[...]).astype(BF16)

    acc = jnp.dot(xn_ref[...], w_ref[...], preferred_element_type=F32)

    @pl.when(j < n_qk_tiles)
    def _():
        gain = jnp.where(j < n_q_tiles, gain_ref[0:1, :], gain_ref[1:2, :])
        rc = rc_ref[...]
        ra = ra_ref[...]
        rb = rb_ref[...]
        for hh in range(TN_QKV // HEAD_DIM):
            xh = acc[:, hh * HEAD_DIM:(hh + 1) * HEAD_DIM]
            y = xh * _rms_scale(xh) * gain
            up = pltpu.roll(y, HEAD_DIM - ROPE_AXIS_DIM // 2, axis=1)
            dn = pltpu.roll(y, ROPE_AXIS_DIM // 2, axis=1)
            o_ref[:, hh * HEAD_DIM:(hh + 1) * HEAD_DIM] = (y * rc + up * ra + dn * rb).astype(BF16)

    @pl.when(j >= n_qk_tiles)
    def _():
        o_ref[...] = acc.astype(BF16)


def _qkv_proj(h, g, w, gains, rc, ra, rb, seq_len):
    t = h.shape[0]
    pos_tiles = seq_len // TM
    pos_map = lambda i, j: (i % pos_tiles, 0)
    return pl.pallas_call(
        _qkv_kernel,
        grid=(t // TM, D_QKV // TN_QKV),
        in_specs=[
            pl.BlockSpec((TM, D_MODEL), lambda i, j: (i, 0)),
            pl.BlockSpec((1, D_MODEL), lambda i, j: (0, 0)),
            pl.BlockSpec((D_MODEL, TN_QKV), lambda i, j: (0, j)),
            pl.BlockSpec((2, HEAD_DIM), lambda i, j: (0, 0)),
            pl.BlockSpec((TM, HEAD_DIM), pos_map),
            pl.BlockSpec((TM, HEAD_DIM), pos_map),
            pl.BlockSpec((TM, HEAD_DIM), pos_map),
        ],
        out_specs=pl.BlockSpec((TM, TN_QKV), lambda i, j: (i, j)),
        out_shape=jax.ShapeDtypeStruct((t, D_QKV), BF16),
        scratch_shapes=[pltpu.VMEM((TM, D_MODEL), BF16)],
        compiler_params=_cparams(("parallel", "arbitrary")),
        name="qkv_proj",
    )(h, g, w, gains, rc, ra, rb)


def _flash_kernel(q_ref, k_ref, v_ref, o_ref, *, seq_len):
    scale = HEAD_DIM ** -0.5
    q = jnp.concatenate([q_ref[:, g * HEAD_DIM:(g + 1) * HEAD_DIM] for g in range(GQ)], axis=0)
    rows = q.shape[0]

    def body(i, carry):
        m, l, acc = carry
        start = pl.multiple_of(i * TK, TK)
        kb = k_ref[pl.ds(start, TK), :]
        vb = v_ref[pl.ds(start, TK), :]
        s = lax.dot_general(q, kb, (((1,), (1,)), ((), ())), preferred_element_type=F32) * scale
        m_new = jnp.maximum(m, jnp.max(s, axis=-1, keepdims=True))
        alpha = jnp.exp(m - m_new)
        p = jnp.exp(s - m_new)
        l = alpha * l + jnp.sum(p, axis=-1, keepdims=True)
        acc = alpha * acc + jnp.dot(p.astype(BF16), vb, preferred_element_type=F32)
        return m_new, l, acc

    init = (jnp.full((rows, 1), -jnp.inf, F32), jnp.zeros((rows, 1), F32),
            jnp.zeros((rows, HEAD_DIM), F32))
    _, l, acc = lax.fori_loop(0, seq_len // TK, body, init)
    out = acc / l
    for g in range(GQ):
        o_ref[:, g * HEAD_DIM:(g + 1) * HEAD_DIM] = out[g * TQ:(g + 1) * TQ].astype(BF16)


def _flash_attention(qkv, seq_len):
    t = qkv.shape[0]
    n_seq = t // seq_len
    q_tiles = seq_len // TQ
    k_col = D_Q // HEAD_DIM
    v_col = (D_Q + D_KV) // HEAD_DIM
    return pl.pallas_call(
        functools.partial(_flash_kernel, seq_len=seq_len),
        grid=(n_seq, N_KV_HEADS, q_tiles),
        in_specs=[
            pl.BlockSpec((TQ, GQ * HEAD_DIM), lambda s, kv, qi: (s * q_tiles + qi, kv)),
            pl.BlockSpec((seq_len, HEAD_DIM), lambda s, kv, qi: (s, k_col + kv)),
            pl.BlockSpec((seq_len, HEAD_DIM), lambda s, kv, qi: (s, v_col + kv)),
        ],
        out_specs=pl.BlockSpec((TQ, GQ * HEAD_DIM), lambda s, kv, qi: (s * q_tiles + qi, kv)),
        out_shape=jax.ShapeDtypeStruct((t, D_Q), BF16),
        compiler_params=_cparams(("parallel", "parallel", "arbitrary")),
        name="flash_attention",
    )(qkv, qkv, qkv)


def _proj_residual_kernel(a_ref, w_ref, h_ref, o_ref):
    o_ref[...] = h_ref[...] + jnp.dot(a_ref[...], w_ref[...], preferred_element_type=F32)


def _proj_residual(a, w, h):
    t, k = a.shape
    return pl.pallas_call(
        _proj_residual_kernel,
        grid=(t // TM, D_MODEL // TN_OUT),
        in_specs=[
            pl.BlockSpec((TM, k), lambda i, j: (i, 0)),
            pl.BlockSpec((k, TN_OUT), lambda i, j: (0, j)),
            pl.BlockSpec((TM, TN_OUT), lambda i, j: (i, j)),
        ],
        out_specs=pl.BlockSpec((TM, TN_OUT), lambda i, j: (i, j)),
        out_shape=jax.ShapeDtypeStruct((t, D_MODEL), F32),
        input_output_aliases={2: 0},
        compiler_params=_cparams(("parallel", "arbitrary")),
        name="proj_residual",
    )(a, w, h)


def _mlp_kernel(h_ref, g_ref, wu_ref, wd_ref, o_ref, xn_ref):
    f = pl.program_id(1)

    @pl.when(f == 0)
    def _():
        x = h_ref[...]
        xn_ref[...] = (x * _rms_scale(x) * g_ref[...]).astype(BF16)
        o_ref[...] = x

    u = jnp.maximum(jnp.dot(xn_ref[...], wu_ref[...], preferred_element_type=F32), 0.0)
    o_ref[...] += jnp.dot((u * u).astype(BF16), wd_ref[...], preferred_element_type=F32)


def _mlp(h, g, w_up, w_down):
    t = h.shape[0]
    return pl.pallas_call(
        _mlp_kernel,
        grid=(t // TM, D_FF // TF),
        in_specs=[
            pl.BlockSpec((TM, D_MODEL), lambda i, f: (i, 0)),
            pl.BlockSpec((1, D_MODEL), lambda i, f: (0, 0)),
            pl.BlockSpec((D_MODEL, TF), lambda i, f: (0, f)),
            pl.BlockSpec((TF, D_MODEL), lambda i, f: (f, 0)),
        ],
        out_specs=pl.BlockSpec((TM, D_MODEL), lambda i, f: (i, 0)),
        out_shape=jax.ShapeDtypeStruct((t, D_MODEL), F32),
        scratch_shapes=[pltpu.VMEM((TM, D_MODEL), BF16)],
        compiler_params=_cparams(("parallel", "arbitrary")),
        name="mlp",
    )(h, g, w_up, w_down)


def _in_proj_kernel(h_ref, g_ref, w_ref, wdt_ref, o_ref, dt_ref, xn_ref):
    j = pl.program_id(1)

    @pl.when(j == 0)
    def _():
        x = h_ref[...]
        xn_ref[...] = (x * _rms_scale(x) * g_ref[...]).astype(BF16)
        dt_ref[...] = jnp.dot(xn_ref[...], wdt_ref[...], preferred_element_type=F32)

    o_ref[...] = jnp.dot(xn_ref[...], w_ref[...], preferred_element_type=F32).astype(BF16)


def _in_proj(h, g, w_main, w_dt):
    t = h.shape[0]
    return pl.pallas_call(
        _in_proj_kernel,
        grid=(t // TM, D_ZXBC // TN_IN),
        in_specs=[
            pl.BlockSpec((TM, D_MODEL), lambda i, j: (i, 0)),
            pl.BlockSpec((1, D_MODEL), lambda i, j: (0, 0)),
            pl.BlockSpec((D_MODEL, TN_IN), lambda i, j: (0, j)),
            pl.BlockSpec((D_MODEL, N_DT), lambda i, j: (0, 0)),
        ],
        out_specs=[
            pl.BlockSpec((TM, TN_IN), lambda i, j: (i, j)),
            pl.BlockSpec((TM, N_DT), lambda i, j: (i, 0)),
        ],
        out_shape=[jax.ShapeDtypeStruct((t, D_ZXBC), BF16),
                   jax.ShapeDtypeStruct((t, N_DT), F32)],
        scratch_shapes=[pltpu.VMEM((TM, D_MODEL), BF16)],
        compiler_params=_cparams(("parallel", "arbitrary")),
        name="ssd_in_proj",
    )(h, g, w_main, w_dt)


def _conv_kernel(prev_ref, cur_ref, next_ref, w_ref, b_ref, o_ref, ext_ref, *, row_tiles):
    r = pl.program_id(1)
    halo = 8
    prev = prev_ref[...].astype(F32)[BF16_SUBLANES - halo:, :]
    nxt = next_ref[...].astype(F32)[:halo, :]
    ext_ref[0:halo, :] = jnp.where(r == 0, 0.0, prev)
    ext_ref[halo:halo + CONV_ROWS, :] = cur_ref[...].astype(F32)
    ext_ref[halo + CONV_ROWS:, :] = jnp.where(r == row_tiles - 1, 0.0, nxt)
    acc = jnp.zeros((CONV_ROWS, CONV_COLS), F32) + b_ref[...]
    for k in range(D_CONV):
        off = halo - CONV_PAD + k
        acc = acc + ext_ref[off:off + CONV_ROWS, :] * w_ref[k:k + 1, :]
    o_ref[...] = (acc * (1.0 / (1.0 + jnp.exp(-acc)))).astype(BF16)


def _conv_silu(zxbc, conv_w, conv_b, seq_len):
    t = zxbc.shape[0]
    n_seq = t // seq_len
    row_tiles = seq_len // CONV_ROWS
    col0 = D_INNER // CONV_COLS
    halo_per_tile = CONV_ROWS // BF16_SUBLANES
    last_halo = t // BF16_SUBLANES - 1

    def cur_map(s, r, c):
        return (s * row_tiles + r, col0 + c)

    def prev_map(s, r, c):
        return (jnp.maximum((s * row_tiles + r) * halo_per_tile - 1, 0), col0 + c)

    def next_map(s, r, c):
        return (jnp.minimum((s * row_tiles + r + 1) * halo_per_tile, last_halo), col0 + c)

    return pl.pallas_call(
        functools.partial(_conv_kernel, row_tiles=row_tiles),
        grid=(n_seq, row_tiles, CONV_DIM // CONV_COLS),
        in_specs=[
            pl.BlockSpec((BF16_SUBLANES, CONV_COLS), prev_map),
            pl.BlockSpec((CONV_ROWS, CONV_COLS), cur_map),
            pl.BlockSpec((BF16_SUBLANES, CONV_COLS), next_map),
            pl.BlockSpec((D_CONV, CONV_COLS), lambda s, r, c: (0, c)),
            pl.BlockSpec((1, CONV_COLS), lambda s, r, c: (0, c)),
        ],
        out_specs=pl.BlockSpec((CONV_ROWS, CONV_COLS), lambda s, r, c: (s * row_tiles + r, c)),
        out_shape=jax.ShapeDtypeStruct((t, CONV_DIM), BF16),
        scratch_shapes=[pltpu.VMEM((CONV_ROWS + 16, CONV_COLS), F32)],
        compiler_params=_cparams(("parallel", "parallel", "arbitrary")),
        name="ssd_conv_silu",
    )(zxbc, zxbc, zxbc, conv_w, conv_b)


def _cumsum_rows(v, reverse):
    n = v.shape[0]
    row = lax.broadcasted_iota(jnp.int32, v.shape, 0)
    k = 1
    while k < n:
        if reverse:
            v = v + jnp.where(row < n - k, pltpu.roll(v, n - k, axis=0), 0.0)
        else:
            v = v + jnp.where(row >= k, pltpu.roll(v, k, axis=0), 0.0)
        k *= 2
    return v


def _split3(v):
    hi = v.astype(BF16)
    r1 = v - hi.astype(F32)
    mid = r1.astype(BF16)
    lo = (r1 - mid.astype(F32)).astype(BF16)
    return jnp.concatenate([hi, mid, lo], axis=1)


def _scan_kernel(*refs, reverse, final):
    if final:
        (x_ref, b_ref, c_ref, dt_ref, bias_ref, alog_ref, sel_ref,
         yf_ref, z_ref, dskip_ref, ng_ref, o_ref, state_ref, cst_ref) = refs
    else:
        (x_ref, b_ref, c_ref, dt_ref, bias_ref, alog_ref, sel_ref,
         o_ref, state_ref, cst_ref) = refs
    g = pl.program_id(1)
    j = pl.program_id(2)
    direction = 1 if reverse else 0
    wide = HEADS_PER_GROUP * CHUNK

    @pl.when(j == 0)
    def _():
        state_ref[...] = jnp.zeros_like(state_ref)

    x = x_ref[...].astype(F32)
    bm = b_ref[...]
    cm = c_ref[...]
    dtr = dt_ref[...] + bias_ref[...]
    dt = jnp.maximum(dtr, 0.0) + jnp.log1p(jnp.exp(-jnp.abs(dtr)))
    da = dt * (-jnp.exp(alog_ref[...]))
    cs = _cumsum_rows(da, reverse)
    end_row = 0 if reverse else CHUNK - 1

    sel = sel_ref[...]
    cs_x = jnp.dot(_split3(cs), sel, preferred_element_type=F32)
    cs64 = cs_x[:, :GROUP_WIDTH]
    cs128 = cs_x[:, GROUP_WIDTH:]
    dt64 = jnp.dot(_split3(dt), sel[:, :GROUP_WIDTH], preferred_element_type=F32)
    cs_end64 = cs64[end_row:end_row + 1, :]

    cst_ref[...] = cs.T
    row0 = pl.multiple_of(direction * SSD_HEADS + g * HEADS_PER_GROUP, HEADS_PER_GROUP)
    cs_rows = cst_ref[pl.ds(row0, HEADS_PER_GROUP), :]

    cb = lax.dot_general(cm, bm, (((1,), (1,)), ((), ())), preferred_element_type=F32)
    li = lax.broadcasted_iota(jnp.int32, (CHUNK, CHUNK), 0)
    si = lax.broadcasted_iota(jnp.int32, (CHUNK, CHUNK), 1)
    mask = (li <= si) if reverse else (li >= si)

    xdt = x * dt64
    xdt_b = xdt.astype(BF16)
    lane_head = lax.broadcasted_iota(jnp.int32, (CHUNK, GROUP_WIDTH), 1) // SSD_HEAD_DIM
    m_parts = []
    x_parts = []
    for hh in range(HEADS_PER_GROUP):
        seg = cs128[:, hh * CHUNK:(hh + 1) * CHUNK] - cs_rows[hh:hh + 1, :]
        decay = jnp.exp(jnp.where(mask, seg, -jnp.inf))
        m_parts.append((cb * decay).astype(BF16))
        x_parts.append(jnp.where(lane_head == hh, xdt_b, jnp.zeros_like(xdt_b)))
    m_cat = jnp.concatenate(m_parts, axis=1)
    x_stack = jnp.concatenate(x_parts, axis=0)
    y = jnp.dot(m_cat, x_stack, preferred_element_type=F32)

    prev = state_ref[...]
    y = y + jnp.dot(cm, prev.astype(BF16), preferred_element_type=F32) * jnp.exp(cs64)

    xw = (xdt * jnp.exp(cs_end64 - cs64)).astype(BF16)
    new_states = lax.dot_general(bm, xw, (((0,), (0,)), ((), ())), preferred_element_type=F32)
    state_ref[...] = jnp.exp(cs_end64) * prev + new_states

    if final:
        y = y + yf_ref[...] + dskip_ref[...] * x
        z = z_ref[...].astype(F32)
        y = y * (z * (1.0 / (1.0 + jnp.exp(-z))))
        o_ref[...] = (y * _rms_scale(y) * ng_ref[...]).astype(BF16)
    else:
        o_ref[...] = y


def _ssd_scan(xbc, dt_raw, dt_bias, a_log, sel, seq_len, final_inputs=None):
    t = xbc.shape[0]
    n_seq = t // seq_len
    nc = seq_len // CHUNK
    reverse = final_inputs is not None
    b_col = D_INNER // D_STATE
    c_col = b_col + SSD_GROUPS

    def row(s, j):
        return s * nc + ((nc - 1 - j) if reverse else j)

    direction = 1 if reverse else 0
    in_specs = [
        pl.BlockSpec((CHUNK, GROUP_WIDTH), lambda s, g, j: (row(s, j), g)),
        pl.BlockSpec((CHUNK, D_STATE), lambda s, g, j: (row(s, j), b_col + g)),
        pl.BlockSpec((CHUNK, D_STATE), lambda s, g, j: (row(s, j), c_col + g)),
        pl.BlockSpec((CHUNK, N_DT), lambda s, g, j: (row(s, j), 0)),
        pl.BlockSpec((1, N_DT), lambda s, g, j: (0, 0)),
        pl.BlockSpec((1, N_DT), lambda s, g, j: (0, 0)),
        pl.BlockSpec((None,) + sel.shape[1:], lambda s, g, j: (direction * SSD_GROUPS + g, 0, 0)),
    ]
    args = [xbc, xbc, xbc, dt_raw, dt_bias, a_log, sel]
    if reverse:
        y_fwd, zxbc, d_skip, norm_g = final_inputs
        in_specs += [
            pl.BlockSpec((CHUNK, GROUP_WIDTH), lambda s, g, j: (row(s, j), g)),
            pl.BlockSpec((CHUNK, GROUP_WIDTH), lambda s, g, j: (row(s, j), g)),
            pl.BlockSpec((1, GROUP_WIDTH), lambda s, g, j: (0, g)),
            pl.BlockSpec((1, GROUP_WIDTH), lambda s, g, j: (0, g)),
        ]
        args += [y_fwd, zxbc, d_skip, norm_g]
    return pl.pallas_call(
        functools.partial(_scan_kernel, reverse=reverse, final=reverse),
        grid=(n_seq, SSD_GROUPS, nc),
        in_specs=in_specs,
        out_specs=pl.BlockSpec((CHUNK, GROUP_WIDTH), lambda s, g, j: (row(s, j), g)),
        out_shape=jax.ShapeDtypeStruct((t, D_INNER), BF16 if reverse else F32),
        scratch_shapes=[pltpu.VMEM((D_STATE, GROUP_WIDTH), F32),
                        pltpu.VMEM((N_DT, CHUNK), F32)],
        compiler_params=_cparams(("parallel", "parallel", "arbitrary")),
        name="ssd_scan_bwd" if reverse else "ssd_scan_fwd",
    )(*args)


def _rope_tables(seq_len):
    rows = seq_len // GRID_W
    row = jnp.repeat(jnp.arange(rows, dtype=F32), GRID_W)
    col = jnp.tile(jnp.arange(GRID_W, dtype=F32), rows)
    inv_freq = ROPE_THETA ** (-jnp.arange(0, ROPE_AXIS_DIM, 2, dtype=F32) / ROPE_AXIS_DIM)
    ang = jnp.stack([row, col], axis=-1)[..., None] * inv_freq
    cos, sin = jnp.cos(ang), jnp.sin(ang)
    zero = jnp.zeros_like(sin)
    rc = jnp.stack([cos, cos], axis=2).reshape(seq_len, HEAD_DIM)
    ra = jnp.stack([-sin, zero], axis=2).reshape(seq_len, HEAD_DIM)
    rb = jnp.stack([zero, sin], axis=2).reshape(seq_len, HEAD_DIM)
    return rc, ra, rb


def _head_selectors():
    col = jnp.arange(N_DT)[None, :, None]
    dg = jnp.arange(2 * SSD_GROUPS)[:, None, None]
    first = (dg // SSD_GROUPS) * SSD_HEADS + (dg % SSD_GROUPS) * HEADS_PER_GROUP
    lane64 = jnp.arange(GROUP_WIDTH)[None, None, :] // SSD_HEAD_DIM
    lane128 = jnp.arange(HEADS_PER_GROUP * CHUNK)[None, None, :] // CHUNK
    e64 = (col == first + lane64)
    e128 = (col == first + lane128)
    e = jnp.concatenate([e64, e128], axis=2).astype(BF16)
    return jnp.concatenate([e, e, e], axis=1)


def kernel(x_prompt, x_sample, norm_mix, norm_mlp, attn_w_qkv, attn_q_norm, attn_k_norm, attn_w_o,
           ssd_w_in, ssd_conv_w, ssd_conv_b, ssd_dt_bias, ssd_a_log, ssd_d, ssd_norm, ssd_w_out,
           mlp_w_up, mlp_w_down):
    seq_len = x_prompt.shape[1]
    assert x_sample.shape[1] == seq_len and x_prompt.shape[2] == D_MODEL
    n_prompt = x_prompt.shape[0] * seq_len
    h = jnp.concatenate([x_prompt.reshape(-1, D_MODEL), x_sample.reshape(-1, D_MODEL)], axis=0)
    depth = norm_mix.shape[0]

    rc, ra, rb = _rope_tables(seq_len)
    sel = _head_selectors()

    for i in range(depth):
        jdx = i // 2
        g_mix = norm_mix[i].reshape(1, D_MODEL)
        if i % 2 == 0:
            gains = jnp.stack([attn_q_norm[jdx], attn_k_norm[jdx]], axis=0)
            qkv = _qkv_proj(h, g_mix, attn_w_qkv[jdx].astype(BF16), gains, rc, ra, rb, seq_len)
            o = _flash_attention(qkv, seq_len)
            h = _proj_residual(o, attn_w_o[jdx].astype(BF16), h)
        else:
            w_in = ssd_w_in[jdx]
            zxbc, dt_raw = _in_proj(h, g_mix, w_in[:, :D_ZXBC].astype(BF16),
                                    w_in[:, D_ZXBC:].astype(BF16))
            xbc = _conv_silu(zxbc, ssd_conv_w[jdx], ssd_conv_b[jdx].reshape(1, CONV_DIM), seq_len)
            dt_bias = ssd_dt_bias[jdx].reshape(1, N_DT)
            a_log = ssd_a_log[jdx].reshape(1, N_DT)
            y_fwd = _ssd_scan(xbc, dt_raw, dt_bias, a_log, sel, seq_len)
            d_skip = jnp.repeat(ssd_d[jdx], SSD_HEAD_DIM).reshape(1, D_INNER)
            y = _ssd_scan(xbc, dt_raw, dt_bias, a_log, sel, seq_len,
                          final_inputs=(y_fwd, zxbc, d_skip, ssd_norm[jdx].reshape(1, D_INNER)))
            h = _proj_residual(y, ssd_w_out[jdx].astype(BF16), h)
        h = _mlp(h, norm_mlp[i].reshape(1, D_MODEL), mlp_w_up[i].astype(BF16),
                 mlp_w_down[i].astype(BF16))

    y_prompt = h[:n_prompt].reshape(x_prompt.shape)
    y_sample = h[n_prompt:].reshape(x_sample.shape)
    return (y_prompt, y_sample)
```

```python
import functools
import math

import jax
import jax.numpy as jnp
from jax import lax
from jax.experimental import pallas as pl
from jax.experimental.pallas import tpu as pltpu

F32 = jnp.float32
BF16 = jnp.bfloat16

D_MODEL = 2048
GRID_W = 64
HEAD_DIM = 128
N_HEADS = D_MODEL // HEAD_DIM
N_KV_HEADS = N_HEADS // 2
GQ = N_HEADS // N_KV_HEADS
D_Q = N_HEADS * HEAD_DIM
D_KV = N_KV_HEADS * HEAD_DIM
D_QKV = D_Q + 2 * D_KV
ROPE_AXIS_DIM = HEAD_DIM // 2
ROPE_THETA = 10000.0
D_INNER = 2 * D_MODEL
SSD_HEAD_DIM = 64
SSD_HEADS = D_INNER // SSD_HEAD_DIM
SSD_GROUPS = 8
HEADS_PER_GROUP = SSD_HEADS // SSD_GROUPS
GROUP_WIDTH = D_INNER // SSD_GROUPS
D_STATE = 128
D_CONV = 5
CONV_PAD = (D_CONV - 1) // 2
CHUNK = 128
CONV_DIM = D_INNER + 2 * SSD_GROUPS * D_STATE
D_ZXBC = D_INNER + CONV_DIM
N_DT = 2 * SSD_HEADS
D_FF = 4 * D_MODEL
NORM_EPS = 1e-6

LANES = 128
BF16_SUBLANES = 16
VMEM_LIMIT = 56 * 1024 * 1024

TM = 512
TN_QKV = 512
TN_IN = 1024
TN_OUT = 1024
TF = 1024
TQ = 256
TK = 256
FLASH_UNROLL = 8
SCAN_CHUNKS = 2
CONV_ROWS = 512
CONV_COLS = 1024


def _cparams(sem):
    return pltpu.CompilerParams(dimension_semantics=sem, vmem_limit_bytes=VMEM_LIMIT)


def _rms_scale(x):
    return lax.rsqrt(jnp.mean(x * x, axis=-1, keepdims=True) + NORM_EPS)


def _qkv_kernel(h_ref, g_ref, w_ref, gain_ref, rc_ref, ra_ref, rb_ref, o_ref, xn_ref):
    j = pl.program_id(1)
    n_q_tiles = D_Q // TN_QKV
    n_qk_tiles = (D_Q + D_KV) // TN_QKV

    @pl.when(j == 0)
    def _():
        x = h_ref[...]
        xn_ref[...] = (x * _rms_scale(x) * g_ref[...]).astype(BF16)

    acc = jnp.dot(xn_ref[...], w_ref[...], preferred_element_type=F32)

    @pl.when(j < n_qk_tiles)
    def _():
        gain = jnp.where(j < n_q_tiles, gain_ref[0:1, :], gain_ref[1:2, :])
        rc = rc_ref[...]
        ra = ra_ref[...]
        rb = rb_ref[...]
        for hh in range(TN_QKV // HEAD_DIM):
            xh = acc[:, hh * HEAD_DIM:(hh + 1) * HEAD_DIM]
            y = xh * _rms_scale(xh) * gain
            up = pltpu.roll(y, HEAD_DIM - ROPE_AXIS_DIM // 2, axis=1)
            dn = pltpu.roll(y, ROPE_AXIS_DIM // 2, axis=1)
            o_ref[:, hh * HEAD_DIM:(hh + 1) * HEAD_DIM] = (y * rc + up * ra + dn * rb).astype(BF16)

    @pl.when(j >= n_qk_tiles)
    def _():
        o_ref[...] = acc.astype(BF16)


def _qkv_proj(h, g, w, gains, rc, ra, rb, seq_len):
    t = h.shape[0]
    pos_tiles = seq_len // TM
    pos_map = lambda i, j: (i % pos_tiles, 0)
    return pl.pallas_call(
        _qkv_kernel,
        grid=(t // TM, D_QKV // TN_QKV),
        in_specs=[
            pl.BlockSpec((TM, D_MODEL), lambda i, j: (i, 0)),
            pl.BlockSpec((1, D_MODEL), lambda i, j: (0, 0)),
            pl.BlockSpec((D_MODEL, TN_QKV), lambda i, j: (0, j)),
            pl.BlockSpec((2, HEAD_DIM), lambda i, j: (0, 0)),
            pl.BlockSpec((TM, HEAD_DIM), pos_map),
            pl.BlockSpec((TM, HEAD_DIM), pos_map),
            pl.BlockSpec((TM, HEAD_DIM), pos_map),
        ],
        out_specs=pl.BlockSpec((TM, TN_QKV), lambda i, j: (i, j)),
        out_shape=jax.ShapeDtypeStruct((t, D_QKV), BF16),
        scratch_shapes=[pltpu.VMEM((TM, D_MODEL), BF16)],
        compiler_params=_cparams(("parallel", "arbitrary")),
        name="qkv_proj",
    )(h, g, w, gains, rc, ra, rb)


def _flash_kernel(bound_ref, q_ref, k_ref, v_ref, o_ref, vt_ref, *, seq_len, bounded):
    qi = pl.program_id(2)
    n_kv = seq_len // TK
    c1 = (HEAD_DIM ** -0.5) * math.log2(math.e)
    nq = GQ * TQ
    sub = 8

    @pl.when(qi == 0)
    def _():
        def fill(i, carry):
            start = pl.multiple_of(i * TK, TK)
            vt_ref[i] = v_ref[pl.ds(start, TK), :].astype(F32).T.astype(BF16)
            return carry
        lax.fori_loop(0, n_kv, fill, 0)

    qt = jnp.concatenate([q_ref[:, g * HEAD_DIM:(g + 1) * HEAD_DIM].astype(F32).T.astype(BF16)
                          for g in range(GQ)], axis=1)

    def scores(i):
        start = pl.multiple_of(i * TK, TK)
        return jnp.dot(k_ref[pl.ds(start, TK), :], qt, preferred_element_type=F32)

    if bounded:
        shift = bound_ref[0, 0]

        def body(i, carry):
            l, acc, st = carry
            for u in range(FLASH_UNROLL):
                t = i * FLASH_UNROLL + u
                st_next = scores(jnp.minimum(t + 1, n_kv - 1))
                p = jnp.exp2(st * c1 - shift)
                l = l + jnp.sum(p.reshape(TK // sub, sub, nq), axis=0)
                acc = acc + jnp.dot(vt_ref[t], p.astype(BF16), preferred_element_type=F32)
                st = st_next
            return l, acc, st

        init = (jnp.zeros((sub, nq), F32), jnp.zeros((HEAD_DIM, nq), F32), scores(0))
        l, acc, _ = lax.fori_loop(0, n_kv // FLASH_UNROLL, body, init)
        l = jnp.sum(l, axis=0, keepdims=True)
    else:
        def body(i, carry):
            m, l, acc = carry
            sc = scores(i) * c1
            m_new = jnp.maximum(m, jnp.max(sc, axis=0, keepdims=True))
            alpha = jnp.exp2(m - m_new)
            p = jnp.exp2(sc - m_new)
            l = alpha * l + jnp.sum(p, axis=0, keepdims=True)
            acc = alpha * acc + jnp.dot(vt_ref[i], p.astype(BF16), preferred_element_type=F32)
            return m_new, l, acc

        init = (jnp.full((1, nq), -jnp.inf, F32), jnp.zeros((1, nq), F32),
                jnp.zeros((HEAD_DIM, nq), F32))
        _, l, acc = lax.fori_loop(0, n_kv, body, init)
    out_t = acc / l
    for g in range(GQ):
        o_ref[:, g * HEAD_DIM:(g + 1) * HEAD_DIM] = out_t[:, g * TQ:(g + 1) * TQ].T.astype(BF16)


def _flash_call(qkv, bound, seq_len, bounded):
    assert seq_len % (TK * FLASH_UNROLL) == 0 and seq_len % TQ == 0
    t = qkv.shape[0]
    n_seq = t // seq_len
    q_tiles = seq_len // TQ
    k_col = D_Q // HEAD_DIM
    v_col = (D_Q + D_KV) // HEAD_DIM
    return pl.pallas_call(
        functools.partial(_flash_kernel, seq_len=seq_len, bounded=bounded),
        grid=(n_seq, N_KV_HEADS, q_tiles),
        in_specs=[
            pl.BlockSpec(memory_space=pltpu.SMEM),
            pl.BlockSpec((TQ, GQ * HEAD_DIM), lambda s, kv, qi: (s * q_tiles + qi, kv)),
            pl.BlockSpec((seq_len, HEAD_DIM), lambda s, kv, qi: (s, k_col + kv)),
            pl.BlockSpec((seq_len, HEAD_DIM), lambda s, kv, qi: (s, v_col + kv)),
        ],
        out_specs=pl.BlockSpec((TQ, GQ * HEAD_DIM), lambda s, kv, qi: (s * q_tiles + qi, kv)),
        out_shape=jax.ShapeDtypeStruct((t, D_Q), BF16),
        scratch_shapes=[pltpu.VMEM((seq_len // TK, HEAD_DIM, TK), BF16)],
        compiler_params=_cparams(("parallel", "parallel", "arbitrary")),
        name="flash_bounded" if bounded else "flash_online",
    )(bound, qkv, qkv, qkv)


FLASH_MAX_BOUND = 60.0


def _flash_attention(qkv, q_gain, k_gain, seq_len):
    c1 = (HEAD_DIM ** -0.5) * math.log2(math.e)
    bound = (c1 * HEAD_DIM * (1.0 + 2.0 ** -7)) * jnp.max(jnp.abs(q_gain)) * jnp.max(jnp.abs(k_gain))
    bound = bound.astype(F32).reshape(1, 1)
    return lax.cond(bound[0, 0] <= FLASH_MAX_BOUND,
                    lambda: _flash_call(qkv, bound, seq_len, True),
                    lambda: _flash_call(qkv, bound, seq_len, False))


def _proj_residual_kernel(a_ref, w_ref, h_ref, o_ref):
    o_ref[...] = h_ref[...] + jnp.dot(a_ref[...], w_ref[...], preferred_element_type=F32)


def _proj_residual(a, w, h):
    t, k = a.shape
    return pl.pallas_call(
        _proj_residual_kernel,
        grid=(t // TM, D_MODEL // TN_OUT),
        in_specs=[
            pl.BlockSpec((TM, k), lambda i, j: (i, 0)),
            pl.BlockSpec((k, TN_OUT), lambda i, j: (0, j)),
            pl.BlockSpec((TM, TN_OUT), lambda i, j: (i, j)),
        ],
        out_specs=pl.BlockSpec((TM, TN_OUT), lambda i, j: (i, j)),
        out_shape=jax.ShapeDtypeStruct((t, D_MODEL), F32),
        input_output_aliases={2: 0},
        compiler_params=_cparams(("parallel", "arbitrary")),
        name="proj_residual",
    )(a, w, h)


def _mlp_kernel(h_ref, g_ref, wu_ref, wd_ref, o_ref, xn_ref):
    f = pl.program_id(1)

    @pl.when(f == 0)
    def _():
        x = h_ref[...]
        xn_ref[...] = (x * _rms_scale(x) * g_ref[...]).astype(BF16)
        o_ref[...] = x

    u = jnp.maximum(jnp.dot(xn_ref[...], wu_ref[...], preferred_element_type=F32), 0.0)
    o_ref[...] += jnp.dot((u * u).astype(BF16), wd_ref[...], preferred_element_type=F32)


def _mlp(h, g, w_up, w_down):
    t = h.shape[0]
    return pl.pallas_call(
        _mlp_kernel,
        grid=(t // TM, D_FF // TF),
        in_specs=[
            pl.BlockSpec((TM, D_MODEL), lambda i, f: (i, 0)),
            pl.BlockSpec((1, D_MODEL), lambda i, f: (0, 0)),
            pl.BlockSpec((D_MODEL, TF), lambda i, f: (0, f)),
            pl.BlockSpec((TF, D_MODEL), lambda i, f: (f, 0)),
        ],
        out_specs=pl.BlockSpec((TM, D_MODEL), lambda i, f: (i, 0)),
        out_shape=jax.ShapeDtypeStruct((t, D_MODEL), F32),
        scratch_shapes=[pltpu.VMEM((TM, D_MODEL), BF16)],
        compiler_params=_cparams(("parallel", "arbitrary")),
        name="mlp",
    )(h, g, w_up, w_down)


def _chunk_cumsum(v, reverse):
    n = v.shape[0]
    pos = lax.broadcasted_iota(jnp.int32, v.shape, 0) % CHUNK
    k = 1
    while k < CHUNK:
        if reverse:
            v = v + jnp.where(pos < CHUNK - k, pltpu.roll(v, n - k, axis=0), 0.0)
        else:
            v = v + jnp.where(pos >= k, pltpu.roll(v, k, axis=0), 0.0)
        k *= 2
    return v


def _in_proj_kernel(h_ref, g_ref, w_ref, wdt_ref, bias_ref, alog_ref, o_ref, dt_ref, cs_ref, xn_ref):
    j = pl.program_id(1)

    @pl.when(j == 0)
    def _():
        x = h_ref[...]
        xn_ref[...] = (x * _rms_scale(x) * g_ref[...]).astype(BF16)
        dtr = jnp.dot(xn_ref[...], wdt_ref[...], preferred_element_type=F32) + bias_ref[...]
        dt = jnp.maximum(dtr, 0.0) + jnp.log1p(jnp.exp(-jnp.abs(dtr)))
        da = dt * (-jnp.exp(alog_ref[...]) * math.log2(math.e))
        backward_cols = lax.broadcasted_iota(jnp.int32, da.shape, 1) >= SSD_HEADS
        dt_ref[...] = dt
        cs_ref[...] = jnp.where(backward_cols, _chunk_cumsum(da, True), _chunk_cumsum(da, False))

    o_ref[...] = jnp.dot(xn_ref[...], w_ref[...], preferred_element_type=F32).astype(BF16)


def _in_proj(h, g, w_main, w_dt, dt_bias, a_log):
    t = h.shape[0]
    row = lambda i, j: (0, 0)
    return pl.pallas_call(
        _in_proj_kernel,
        grid=(t // TM, D_ZXBC // TN_IN),
        in_specs=[
            pl.BlockSpec((TM, D_MODEL), lambda i, j: (i, 0)),
            pl.BlockSpec((1, D_MODEL), row),
            pl.BlockSpec((D_MODEL, TN_IN), lambda i, j: (0, j)),
            pl.BlockSpec((D_MODEL, N_DT), row),
            pl.BlockSpec((1, N_DT), row),
            pl.BlockSpec((1, N_DT), row),
        ],
        out_specs=[
            pl.BlockSpec((TM, TN_IN), lambda i, j: (i, j)),
            pl.BlockSpec((TM, N_DT), lambda i, j: (i, 0)),
            pl.BlockSpec((TM, N_DT), lambda i, j: (i, 0)),
        ],
        out_shape=[jax.ShapeDtypeStruct((t, D_ZXBC), BF16),
                   jax.ShapeDtypeStruct((t, N_DT), F32),
                   jax.ShapeDtypeStruct((t, N_DT), F32)],
        scratch_shapes=[pltpu.VMEM((TM, D_MODEL), BF16)],
        compiler_params=_cparams(("parallel", "arbitrary")),
        name="ssd_in_proj",
    )(h, g, w_main, w_dt, dt_bias, a_log)


def _conv_kernel(prev_ref, cur_ref, next_ref, w_ref, b_ref, o_ref, ext_ref, *, row_tiles):
    r = pl.program_id(1)
    halo = 8
    prev = prev_ref[...].astype(F32)[BF16_SUBLANES - halo:, :]
    nxt = next_ref[...].astype(F32)[:halo, :]
    ext_ref[0:halo, :] = jnp.where(r == 0, 0.0, prev)
    ext_ref[halo:halo + CONV_ROWS, :] = cur_ref[...].astype(F32)
    ext_ref[halo + CONV_ROWS:, :] = jnp.where(r == row_tiles - 1, 0.0, nxt)
    acc = jnp.zeros((CONV_ROWS, CONV_COLS), F32) + b_ref[...]
    for k in range(D_CONV):
        off = halo - CONV_PAD + k
        acc = acc + ext_ref[off:off + CONV_ROWS, :] * w_ref[k:k + 1, :]
    o_ref[...] = (acc * (1.0 / (1.0 + jnp.exp(-acc)))).astype(BF16)


def _conv_silu(zxbc, conv_w, conv_b, seq_len):
    t = zxbc.shape[0]
    n_seq = t // seq_len
    row_tiles = seq_len // CONV_ROWS
    col0 = D_INNER // CONV_COLS
    halo_per_tile = CONV_ROWS // BF16_SUBLANES
    last_halo = t // BF16_SUBLANES - 1

    def cur_map(s, r, c):
        return (s * row_tiles + r, col0 + c)

    def prev_map(s, r, c):
        return (jnp.maximum((s * row_tiles + r) * halo_per_tile - 1, 0), col0 + c)

    def next_map(s, r, c):
        return (jnp.minimum((s * row_tiles + r + 1) * halo_per_tile, last_halo), col0 + c)

    return pl.pallas_call(
        functools.partial(_conv_kernel, row_tiles=row_tiles),
        grid=(n_seq, row_tiles, CONV_DIM // CONV_COLS),
        in_specs=[
            pl.BlockSpec((BF16_SUBLANES, CONV_COLS), prev_map),
            pl.BlockSpec((CONV_ROWS, CONV_COLS), cur_map),
            pl.BlockSpec((BF16_SUBLANES, CONV_COLS), next_map),
            pl.BlockSpec((D_CONV, CONV_COLS), lambda s, r, c: (0, c)),
            pl.BlockSpec((1, CONV_COLS), lambda s, r, c: (0, c)),
        ],
        out_specs=pl.BlockSpec((CONV_ROWS, CONV_COLS), lambda s, r, c: (s * row_tiles + r, c)),
        out_shape=jax.ShapeDtypeStruct((t, CONV_DIM), BF16),
        scratch_shapes=[pltpu.VMEM((CONV_ROWS + 16, CONV_COLS), F32)],
        compiler_params=_cparams(("parallel", "parallel", "arbitrary")),
        name="ssd_conv_silu",
    )(zxbc, zxbc, zxbc, conv_w, conv_b)


def _bf16_split(v, parts):
    pieces = []
    for _ in range(parts):
        p = v.astype(BF16)
        pieces.append(p)
        v = v - p.astype(F32)
    return jnp.concatenate(pieces, axis=1)


def _scan_kernel(*refs, reverse, final):
    if final:
        (x_ref, b_ref, c_ref, dt_ref, cs_ref, sel_ref,
         yf_ref, z_ref, dskip_ref, ng_ref, o_ref, state_ref, cst_ref) = refs
    else:
        (x_ref, b_ref, c_ref, dt_ref, cs_ref, sel_ref, o_ref, state_ref, cst_ref) = refs
    g = pl.program_id(1)
    j = pl.program_id(2)
    direction = 1 if reverse else 0
    head_pairs = HEADS_PER_GROUP // 2
    half = LANES // 2

    @pl.when(j == 0)
    def _():
        state_ref[...] = jnp.zeros_like(state_ref)

    sel = sel_ref[...]
    li = lax.broadcasted_iota(jnp.int32, (CHUNK, CHUNK), 0)
    si = lax.broadcasted_iota(jnp.int32, (CHUNK, CHUNK), 1)
    mask = (li <= si) if reverse else (li >= si)
    low_lanes = lax.broadcasted_iota(jnp.int32, (1, LANES), 1) < half
    end_row = 0 if reverse else CHUNK - 1
    row0 = pl.multiple_of(direction * SSD_HEADS + g * HEADS_PER_GROUP, HEADS_PER_GROUP)

    prev = state_ref[...]
    order = range(SCAN_CHUNKS - 1, -1, -1) if reverse else range(SCAN_CHUNKS)
    for u in order:
        rows = slice(u * CHUNK, (u + 1) * CHUNK)
        x = x_ref[rows, :].astype(F32)
        bm = b_ref[rows, :]
        cm = c_ref[rows, :]
        cs = cs_ref[rows, :]

        cs64 = jnp.dot(_bf16_split(cs, 3), sel, preferred_element_type=F32)
        dt64 = jnp.dot(_bf16_split(dt_ref[rows, :], 2), sel[:2 * N_DT, :], preferred_element_type=F32)
        cs_end64 = cs64[end_row:end_row + 1, :]

        cst_ref[u] = cs.T
        cs_rows = cst_ref[u, pl.ds(row0, HEADS_PER_GROUP), :]

        cb = lax.dot_general(cm, bm, (((1,), (1,)), ((), ())), preferred_element_type=F32)
        xdt = x * dt64
        xdt_b = xdt.astype(BF16)
        zero_b = jnp.zeros((CHUNK, LANES), BF16)

        y_parts = []
        for pr in range(head_pairs):
            cols = slice(pr * LANES, (pr + 1) * LANES)
            v = cs64[:, cols]
            r = pltpu.roll(v, half, axis=1)
            m_pair = []
            for hh, cs_col in ((2 * pr, jnp.where(low_lanes, v, r)), (2 * pr + 1, jnp.where(low_lanes, r, v))):
                seg = cs_col - cs_rows[hh:hh + 1, :]
                decay = jnp.exp2(jnp.where(mask, seg, -jnp.inf))
                m_pair.append((cb * decay).astype(BF16))
            xp = xdt_b[:, cols]
            x_pair = jnp.concatenate([jnp.where(low_lanes, xp, zero_b), jnp.where(low_lanes, zero_b, xp)], axis=0)
            y_parts.append(jnp.dot(jnp.concatenate(m_pair, axis=1), x_pair, preferred_element_type=F32))
        y = jnp.concatenate(y_parts, axis=1)

        y = y + jnp.dot(cm, prev.astype(BF16), preferred_element_type=F32) * jnp.exp2(cs64)

        xw = (xdt * jnp.exp2(cs_end64 - cs64)).astype(BF16)
        new_states = lax.dot_general(bm, xw, (((0,), (0,)), ((), ())), preferred_element_type=F32)
        prev = jnp.exp2(cs_end64) * prev + new_states

        if final:
            y = y + yf_ref[rows, :] + dskip_ref[...] * x
            z = z_ref[rows, :].astype(F32)
            y = y * (z * (1.0 / (1.0 + jnp.exp(-z))))
            o_ref[rows, :] = (y * _rms_scale(y) * ng_ref[...]).astype(BF16)
        else:
            o_ref[rows, :] = y
    state_ref[...] = prev


def _ssd_scan(xbc, dt, cs, sel, seq_len, final_inputs=None):
    t = xbc.shape[0]
    n_seq = t // seq_len
    rows = SCAN_CHUNKS * CHUNK
    steps = seq_len // rows
    reverse = final_inputs is not None
    b_col = D_INNER // D_STATE
    c_col = b_col + SSD_GROUPS

    def row(s, j):
        return s * steps + ((steps - 1 - j) if reverse else j)

    direction = 1 if reverse else 0
    in_specs = [
        pl.BlockSpec((rows, GROUP_WIDTH), lambda s, g, j: (row(s, j), g)),
        pl.BlockSpec((rows, D_STATE), lambda s, g, j: (row(s, j), b_col + g)),
        pl.BlockSpec((rows, D_STATE), lambda s, g, j: (row(s, j), c_col + g)),
        pl.BlockSpec((rows, N_DT), lambda s, g, j: (row(s, j), 0)),
        pl.BlockSpec((rows, N_DT), lambda s, g, j: (row(s, j), 0)),
        pl.BlockSpec((None,) + sel.shape[1:], lambda s, g, j: (direction * SSD_GROUPS + g, 0, 0)),
    ]
    args = [xbc, xbc, xbc, dt, cs, sel]
    if reverse:
        y_fwd, zxbc, d_skip, norm_g = final_inputs
        in_specs += [
            pl.BlockSpec((rows, GROUP_WIDTH), lambda s, g, j: (row(s, j), g)),
            pl.BlockSpec((rows, GROUP_WIDTH), lambda s, g, j: (row(s, j), g)),
            pl.BlockSpec((1, GROUP_WIDTH), lambda s, g, j: (0, g)),
            pl.BlockSpec((1, GROUP_WIDTH), lambda s, g, j: (0, g)),
        ]
        args += [y_fwd, zxbc, d_skip, norm_g]
    return pl.pallas_call(
        functools.partial(_scan_kernel, reverse=reverse, final=reverse),
        grid=(n_seq, SSD_GROUPS, steps),
        in_specs=in_specs,
        out_specs=pl.BlockSpec((rows, GROUP_WIDTH), lambda s, g, j: (row(s, j), g)),
        out_shape=jax.ShapeDtypeStruct((t, D_INNER), BF16 if reverse else F32),
        scratch_shapes=[pltpu.VMEM((D_STATE, GROUP_WIDTH), F32),
                        pltpu.VMEM((SCAN_CHUNKS, N_DT, CHUNK), F32)],
        compiler_params=_cparams(("parallel", "parallel", "arbitrary")),
        name="ssd_scan_bwd" if reverse else "ssd_scan_fwd",
    )(*args)


def _rope_tables(seq_len):
    rows = seq_len // GRID_W
    row = jnp.repeat(jnp.arange(rows, dtype=F32), GRID_W)
    col = jnp.tile(jnp.arange(GRID_W, dtype=F32), rows)
    inv_freq = ROPE_THETA ** (-jnp.arange(0, ROPE_AXIS_DIM, 2, dtype=F32) / ROPE_AXIS_DIM)
    ang = jnp.stack([row, col], axis=-1)[..., None] * inv_freq
    cos, sin = jnp.cos(ang), jnp.sin(ang)
    zero = jnp.zeros_like(sin)
    rc = jnp.stack([cos, cos], axis=2).reshape(seq_len, HEAD_DIM)
    ra = jnp.stack([-sin, zero], axis=2).reshape(seq_len, HEAD_DIM)
    rb = jnp.stack([zero, sin], axis=2).reshape(seq_len, HEAD_DIM)
    return rc, ra, rb


def _head_selectors():
    col = jnp.arange(N_DT)[None, :, None]
    dg = jnp.arange(2 * SSD_GROUPS)[:, None, None]
    first = (dg // SSD_GROUPS) * SSD_HEADS + (dg % SSD_GROUPS) * HEADS_PER_GROUP
    lane_head = jnp.arange(GROUP_WIDTH)[None, None, :] // SSD_HEAD_DIM
    e = (col == first + lane_head).astype(BF16)
    return jnp.concatenate([e, e, e], axis=1)


def kernel(x_prompt, x_sample, norm_mix, norm_mlp, attn_w_qkv, attn_q_norm, attn_k_norm, attn_w_o,
           ssd_w_in, ssd_conv_w, ssd_conv_b, ssd_dt_bias, ssd_a_log, ssd_d, ssd_norm, ssd_w_out,
           mlp_w_up, mlp_w_down):
    seq_len = x_prompt.shape[1]
    assert x_sample.shape[1] == seq_len and x_prompt.shape[2] == D_MODEL
    n_prompt = x_prompt.shape[0] * seq_len
    h = jnp.concatenate([x_prompt.reshape(-1, D_MODEL), x_sample.reshape(-1, D_MODEL)], axis=0)
    depth = norm_mix.shape[0]

    rc, ra, rb = _rope_tables(seq_len)
    sel = _head_selectors()

    for i in range(depth):
        jdx = i // 2
        g_mix = norm_mix[i].reshape(1, D_MODEL)
        if i % 2 == 0:
            gains = jnp.stack([attn_q_norm[jdx], attn_k_norm[jdx]], axis=0)
            qkv = _qkv_proj(h, g_mix, attn_w_qkv[jdx].astype(BF16), gains, rc, ra, rb, seq_len)
            o = _flash_attention(qkv, attn_q_norm[jdx], attn_k_norm[jdx], seq_len)
            h = _proj_residual(o, attn_w_o[jdx].astype(BF16), h)
        else:
            w_in = ssd_w_in[jdx]
            zxbc, dt, cs = _in_proj(h, g_mix, w_in[:, :D_ZXBC].astype(BF16),
                                    w_in[:, D_ZXBC:].astype(BF16),
                                    ssd_dt_bias[jdx].reshape(1, N_DT), ssd_a_log[jdx].reshape(1, N_DT))
            xbc = _conv_silu(zxbc, ssd_conv_w[jdx], ssd_conv_b[jdx].reshape(1, CONV_DIM), seq_len)
            y_fwd = _ssd_scan(xbc, dt, cs, sel, seq_len)
            d_skip = jnp.repeat(ssd_d[jdx], SSD_HEAD_DIM).reshape(1, D_INNER)
            y = _ssd_scan(xbc, dt, cs, sel, seq_len,
                          final_inputs=(y_fwd, zxbc, d_skip, ssd_norm[jdx].reshape(1, D_INNER)))
            h = _proj_residual(y, ssd_w_out[jdx].astype(BF16), h)
        h = _mlp(h, norm_mlp[i].reshape(1, D_MODEL), mlp_w_up[i].astype(BF16),
                 mlp_w_down[i].astype(BF16))

    y_prompt = h[:n_prompt].reshape(x_prompt.shape)
    y_sample = h[n_prompt:].reshape(x_sample.shape)
    return (y_prompt, y_sample)
```

```python
import functools
import math

import jax
import jax.numpy as jnp
from jax import lax
from jax.experimental import pallas as pl
from jax.experimental.pallas import tpu as pltpu

F32 = jnp.float32
BF16 = jnp.bfloat16

D_MODEL = 2048
GRID_W = 64
HEAD_DIM = 128
N_HEADS = D_MODEL // HEAD_DIM
N_KV_HEADS = N_HEADS // 2
GQ = N_HEADS // N_KV_HEADS
D_Q = N_HEADS * HEAD_DIM
D_KV = N_KV_HEADS * HEAD_DIM
D_QKV = D_Q + 2 * D_KV
ROPE_AXIS_DIM = HEAD_DIM // 2
ROPE_THETA = 10000.0
D_INNER = 2 * D_MODEL
SSD_HEAD_DIM = 64
SSD_HEADS = D_INNER // SSD_HEAD_DIM
SSD_GROUPS = 8
HEADS_PER_GROUP = SSD_HEADS // SSD_GROUPS
GROUP_WIDTH = D_INNER // SSD_GROUPS
D_STATE = 128
D_CONV = 5
CONV_PAD = (D_CONV - 1) // 2
CHUNK = 128
CONV_DIM = D_INNER + 2 * SSD_GROUPS * D_STATE
D_ZXBC = D_INNER + CONV_DIM
N_DT = 2 * SSD_HEADS
D_FF = 4 * D_MODEL
NORM_EPS = 1e-6

LANES = 128
BF16_SUBLANES = 16
VMEM_LIMIT = 56 * 1024 * 1024

TM = 512
TN_QKV = 512
TN_IN = 2048
TN_OUT = 1024
TF = 1024
TQ = 512
TK = 256
TKV = 1024
SCAN_CHUNKS = 2
CONV_ROWS = 512
CONV_COLS = 1024


def _cparams(sem):
    return pltpu.CompilerParams(dimension_semantics=sem, vmem_limit_bytes=VMEM_LIMIT)


def _rms_scale(x):
    return lax.rsqrt(jnp.mean(x * x, axis=-1, keepdims=True) + NORM_EPS)


def _qkv_kernel(h_ref, g_ref, w_ref, gain_ref, rc_ref, rs_ref, o_ref):
    x = h_ref[...]
    xn = (x * _rms_scale(x) * g_ref[...]).astype(BF16)
    rc = rc_ref[...]
    rs = rs_ref[...]
    for c in range(D_QKV // TN_QKV):
        cols = slice(c * TN_QKV, (c + 1) * TN_QKV)
        acc = jnp.dot(xn, w_ref[:, cols], preferred_element_type=F32)
        if c * TN_QKV >= D_Q + D_KV:
            o_ref[:, cols] = acc.astype(BF16)
            continue
        gain = gain_ref[0:1, :] if c * TN_QKV < D_Q else gain_ref[1:2, :]
        for hh in range(TN_QKV // HEAD_DIM):
            xh = acc[:, hh * HEAD_DIM:(hh + 1) * HEAD_DIM]
            y = xh * _rms_scale(xh) * gain
            rot = pltpu.roll(y, HEAD_DIM // 2, axis=1)
            o_ref[:, c * TN_QKV + hh * HEAD_DIM:c * TN_QKV + (hh + 1) * HEAD_DIM] = (
                y * rc + rot * rs).astype(BF16)


def _qkv_proj(h, g, w, gains, rc, rs, seq_len):
    t = h.shape[0]
    pos_tiles = seq_len // TM
    pos_map = lambda i: (i % pos_tiles, 0)
    return pl.pallas_call(
        _qkv_kernel,
        grid=(t // TM,),
        in_specs=[
            pl.BlockSpec((TM, D_MODEL), lambda i: (i, 0)),
            pl.BlockSpec((1, D_MODEL), lambda i: (0, 0)),
            pl.BlockSpec((D_MODEL, D_QKV), lambda i: (0, 0), pipeline_mode=pl.Buffered(1)),
            pl.BlockSpec((2, HEAD_DIM), lambda i: (0, 0)),
            pl.BlockSpec((TM, HEAD_DIM), pos_map),
            pl.BlockSpec((TM, HEAD_DIM), pos_map),
        ],
        out_specs=pl.BlockSpec((TM, D_QKV), lambda i: (i, 0)),
        out_shape=jax.ShapeDtypeStruct((t, D_QKV), BF16),
        compiler_params=_cparams(("parallel",)),
        name="qkv_proj",
    )(h, g, w, gains, rc, rs)


def _flash_kernel(q_ref, k_ref, v_ref, o_ref, vt_ref, *, seq_len, unshifted):
    qi = pl.program_id(2)
    n_groups = seq_len // TKV
    nq = GQ * TQ
    sub = 8

    @pl.when(qi == 0)
    def _():
        def fill(i, carry):
            start = pl.multiple_of(i * TKV, TKV)
            vt_ref[i] = v_ref[pl.ds(start, TKV), :].astype(F32).T.astype(BF16)
            return carry
        lax.fori_loop(0, n_groups, fill, 0)

    qt = jnp.concatenate([q_ref[:, g * HEAD_DIM:(g + 1) * HEAD_DIM].astype(F32).T.astype(BF16)
                          for g in range(GQ)], axis=1)

    if unshifted:
        def scores(t):
            return jnp.dot(k_ref[t * TK:(t + 1) * TK, :], qt, preferred_element_type=F32)

        tiles_per_group = TKV // TK
        n_kv = seq_len // TK
        l = jnp.zeros((sub, nq), F32)
        acc = jnp.zeros((HEAD_DIM, nq), F32)
        st = scores(0)
        p_group = []
        for t in range(n_kv):
            st_next = scores(t + 1) if t + 1 < n_kv else None
            p = jnp.exp2(st)
            l = l + jnp.sum(p.reshape(TK // sub, sub, nq), axis=0)
            p_group.append(p.astype(BF16))
            if len(p_group) == tiles_per_group:
                acc = acc + jnp.dot(vt_ref[t // tiles_per_group], jnp.concatenate(p_group, axis=0),
                                    preferred_element_type=F32)
                p_group = []
            st = st_next
        l = jnp.sum(l, axis=0, keepdims=True)
    else:
        def body(i, carry):
            m, l, acc = carry
            start = pl.multiple_of(i * TKV, TKV)
            sc = jnp.dot(k_ref[pl.ds(start, TKV), :], qt, preferred_element_type=F32)
            m_new = jnp.maximum(m, jnp.max(sc, axis=0, keepdims=True))
            alpha = jnp.exp2(m - m_new)
            p = jnp.exp2(sc - m_new)
            l = alpha * l + jnp.sum(p, axis=0, keepdims=True)
            acc = alpha * acc + jnp.dot(vt_ref[i], p.astype(BF16), preferred_element_type=F32)
            return m_new, l, acc

        init = (jnp.full((1, nq), -jnp.inf, F32), jnp.zeros((1, nq), F32),
                jnp.zeros((HEAD_DIM, nq), F32))
        _, l, acc = lax.fori_loop(0, n_groups, body, init)
    out_t = acc / l
    for g in range(GQ):
        o_ref[:, g * HEAD_DIM:(g + 1) * HEAD_DIM] = out_t[:, g * TQ:(g + 1) * TQ].T.astype(BF16)


def _flash_call(qkv, seq_len, unshifted):
    assert seq_len % TKV == 0 and TKV % TK == 0 and seq_len % TQ == 0
    t = qkv.shape[0]
    n_seq = t // seq_len
    q_tiles = seq_len // TQ
    k_col = D_Q // HEAD_DIM
    v_col = (D_Q + D_KV) // HEAD_DIM
    return pl.pallas_call(
        functools.partial(_flash_kernel, seq_len=seq_len, unshifted=unshifted),
        grid=(n_seq, N_KV_HEADS, q_tiles),
        in_specs=[
            pl.BlockSpec((TQ, GQ * HEAD_DIM), lambda s, kv, qi: (s * q_tiles + qi, kv)),
            pl.BlockSpec((seq_len, HEAD_DIM), lambda s, kv, qi: (s, k_col + kv)),
            pl.BlockSpec((seq_len, HEAD_DIM), lambda s, kv, qi: (s, v_col + kv)),
        ],
        out_specs=pl.BlockSpec((TQ, GQ * HEAD_DIM), lambda s, kv, qi: (s * q_tiles + qi, kv)),
        out_shape=jax.ShapeDtypeStruct((t, D_Q), BF16),
        scratch_shapes=[pltpu.VMEM((seq_len // TKV, HEAD_DIM, TKV), BF16)],
        compiler_params=_cparams(("parallel", "parallel", "arbitrary")),
        name="flash_unshifted" if unshifted else "flash_online",
    )(qkv, qkv, qkv)


FLASH_MAX_BOUND = 60.0
QK_LOG2_SCALE = (HEAD_DIM ** -0.5) * math.log2(math.e)


def _flash_attention(qkv, q_gain, k_gain, seq_len):
    bound = (QK_LOG2_SCALE * HEAD_DIM * (1.0 + 2.0 ** -7)) * (
        jnp.max(jnp.abs(q_gain)) * jnp.max(jnp.abs(k_gain)))
    return lax.cond(bound <= FLASH_MAX_BOUND,
                    lambda: _flash_call(qkv, seq_len, True),
                    lambda: _flash_call(qkv, seq_len, False))


def _proj_residual_kernel(a_ref, w_ref, h_ref, o_ref):
    o_ref[...] = h_ref[...] + jnp.dot(a_ref[...], w_ref[...], preferred_element_type=F32)


def _proj_residual(a, w, h):
    t, k = a.shape
    return pl.pallas_call(
        _proj_residual_kernel,
        grid=(t // TM, D_MODEL // TN_OUT),
        in_specs=[
            pl.BlockSpec((TM, k), lambda i, j: (i, 0)),
            pl.BlockSpec((k, TN_OUT), lambda i, j: (0, j)),
            pl.BlockSpec((TM, TN_OUT), lambda i, j: (i, j)),
        ],
        out_specs=pl.BlockSpec((TM, TN_OUT), lambda i, j: (i, j)),
        out_shape=jax.ShapeDtypeStruct((t, D_MODEL), F32),
        input_output_aliases={2: 0},
        compiler_params=_cparams(("parallel", "arbitrary")),
        name="proj_residual",
    )(a, w, h)


def _mlp_kernel(h_ref, g_ref, wu_ref, wd_ref, o_ref, xn_ref):
    f = pl.program_id(1)

    @pl.when(f == 0)
    def _():
        x = h_ref[...]
        xn_ref[...] = (x * _rms_scale(x) * g_ref[...]).astype(BF16)
        o_ref[...] = x

    u = jnp.maximum(jnp.dot(xn_ref[...], wu_ref[...], preferred_element_type=F32), 0.0)
    o_ref[...] += jnp.dot((u * u).astype(BF16), wd_ref[...], preferred_element_type=F32)


def _mlp(h, g, w_up, w_down):
    t = h.shape[0]
    return pl.pallas_call(
        _mlp_kernel,
        grid=(t // TM, D_FF // TF),
        in_specs=[
            pl.BlockSpec((TM, D_MODEL), lambda i, f: (i, 0)),
            pl.BlockSpec((1, D_MODEL), lambda i, f: (0, 0)),
            pl.BlockSpec((D_MODEL, TF), lambda i, f: (0, f)),
            pl.BlockSpec((TF, D_MODEL), lambda i, f: (f, 0)),
        ],
        out_specs=pl.BlockSpec((TM, D_MODEL), lambda i, f: (i, 0)),
        out_shape=jax.ShapeDtypeStruct((t, D_MODEL), F32),
        scratch_shapes=[pltpu.VMEM((TM, D_MODEL), BF16)],
        compiler_params=_cparams(("parallel", "arbitrary")),
        name="mlp",
    )(h, g, w_up, w_down)


def _chunk_cumsum(v, reverse):
    n = v.shape[0]
    pos = lax.broadcasted_iota(jnp.int32, v.shape, 0) % CHUNK
    k = 1
    while k < CHUNK:
        if reverse:
            v = v + jnp.where(pos < CHUNK - k, pltpu.roll(v, n - k, axis=0), 0.0)
        else:
            v = v + jnp.where(pos >= k, pltpu.roll(v, k, axis=0), 0.0)
        k *= 2
    return v


def _in_proj_kernel(h_ref, g_ref, w_ref, wdt_ref, bias_ref, alog_ref, o_ref, dt_ref, cs_ref, xn_ref):
    j = pl.program_id(1)

    @pl.when(j == 0)
    def _():
        x = h_ref[...]
        xn_ref[...] = (x * _rms_scale(x) * g_ref[...]).astype(BF16)
        dtr = jnp.dot(xn_ref[...], wdt_ref[...], preferred_element_type=F32) + bias_ref[...]
        dt = jnp.maximum(dtr, 0.0) + jnp.log1p(jnp.exp(-jnp.abs(dtr)))
        da = dt * (-jnp.exp(alog_ref[...]) * math.log2(math.e))
        backward_cols = lax.broadcasted_iota(jnp.int32, da.shape, 1) >= SSD_HEADS
        dt_ref[...] = dt
        cs_ref[...] = jnp.where(backward_cols, _chunk_cumsum(da, True), _chunk_cumsum(da, False))

    o_ref[...] = jnp.dot(xn_ref[...], w_ref[...], preferred_element_type=F32).astype(BF16)


def _in_proj(h, g, w_main, w_dt, dt_bias, a_log):
    t = h.shape[0]
    row = lambda i, j: (0, 0)
    return pl.pallas_call(
        _in_proj_kernel,
        grid=(t // TM, D_ZXBC // TN_IN),
        in_specs=[
            pl.BlockSpec((TM, D_MODEL), lambda i, j: (i, 0)),
            pl.BlockSpec((1, D_MODEL), row),
            pl.BlockSpec((D_MODEL, TN_IN), lambda i, j: (0, j)),
            pl.BlockSpec((D_MODEL, N_DT), row),
            pl.BlockSpec((1, N_DT), row),
            pl.BlockSpec((1, N_DT), row),
        ],
        out_specs=[
            pl.BlockSpec((TM, TN_IN), lambda i, j: (i, j)),
            pl.BlockSpec((TM, N_DT), lambda i, j: (i, 0)),
            pl.BlockSpec((TM, N_DT), lambda i, j: (i, 0)),
        ],
        out_shape=[jax.ShapeDtypeStruct((t, D_ZXBC), BF16),
                   jax.ShapeDtypeStruct((t, N_DT), F32),
                   jax.ShapeDtypeStruct((t, N_DT), F32)],
        scratch_shapes=[pltpu.VMEM((TM, D_MODEL), BF16)],
        compiler_params=_cparams(("parallel", "arbitrary")),
        name="ssd_in_proj",
    )(h, g, w_main, w_dt, dt_bias, a_log)


def _conv_kernel(prev_ref, cur_ref, next_ref, shift_ref, w_ref, b_ref, o_ref, ext_ref, *, row_tiles):
    r = pl.program_id(1)
    halo = BF16_SUBLANES
    zero = jnp.zeros((halo, CONV_COLS), BF16)
    ext_ref[0:halo, :] = jnp.where(r == 0, zero, prev_ref[...])
    ext_ref[halo:halo + CONV_ROWS, :] = cur_ref[...]
    ext_ref[halo + CONV_ROWS:, :] = jnp.where(r == row_tiles - 1, zero, next_ref[...])
    side_taps = [k for k in range(D_CONV) if k != CONV_PAD]
    for blk in range(CONV_ROWS // CHUNK):
        lo = blk * CHUNK
        window = ext_ref[lo:lo + CHUNK + 2 * halo, :]
        shifted = jnp.dot(shift_ref[...], window, preferred_element_type=F32)
        acc = b_ref[...] + ext_ref[lo + halo:lo + halo + CHUNK, :].astype(F32) * w_ref[CONV_PAD:CONV_PAD + 1, :]
        for i, k in enumerate(side_taps):
            acc = acc + shifted[i * CHUNK:(i + 1) * CHUNK, :] * w_ref[k:k + 1, :]
        o_ref[lo:lo + CHUNK, :] = (acc * (1.0 / (1.0 + jnp.exp(-acc)))).astype(BF16)


def _conv_shift_matrix():
    halo = BF16_SUBLANES
    t = jnp.arange(CHUNK)[:, None]
    j = jnp.arange(CHUNK + 2 * halo)[None, :]
    blocks = [(j == t + halo + (k - CONV_PAD)) for k in range(D_CONV) if k != CONV_PAD]
    return jnp.concatenate(blocks, axis=0).astype(BF16)


def _conv_silu(zxbc, conv_w, conv_b, seq_len):
    t = zxbc.shape[0]
    n_seq = t // seq_len
    row_tiles = seq_len // CONV_ROWS
    col0 = D_INNER // CONV_COLS
    halo_per_tile = CONV_ROWS // BF16_SUBLANES
    last_halo = t // BF16_SUBLANES - 1
    shift = _conv_shift_matrix()

    def cur_map(s, r, c):
        return (s * row_tiles + r, col0 + c)

    def prev_map(s, r, c):
        return (jnp.maximum((s * row_tiles + r) * halo_per_tile - 1, 0), col0 + c)

    def next_map(s, r, c):
        return (jnp.minimum((s * row_tiles + r + 1) * halo_per_tile, last_halo), col0 + c)

    return pl.pallas_call(
        functools.partial(_conv_kernel, row_tiles=row_tiles),
        grid=(n_seq, row_tiles, CONV_DIM // CONV_COLS),
        in_specs=[
            pl.BlockSpec((BF16_SUBLANES, CONV_COLS), prev_map),
            pl.BlockSpec((CONV_ROWS, CONV_COLS), cur_map),
            pl.BlockSpec((BF16_SUBLANES, CONV_COLS), next_map),
            pl.BlockSpec(shift.shape, lambda s, r, c: (0, 0)),
            pl.BlockSpec((D_CONV, CONV_COLS), lambda s, r, c: (0, c)),
            pl.BlockSpec((1, CONV_COLS), lambda s, r, c: (0, c)),
        ],
        out_specs=pl.BlockSpec((CONV_ROWS, CONV_COLS), lambda s, r, c: (s * row_tiles + r, c)),
        out_shape=jax.ShapeDtypeStruct((t, CONV_DIM), BF16),
        scratch_shapes=[pltpu.VMEM((CONV_ROWS + 2 * BF16_SUBLANES, CONV_COLS), BF16)],
        compiler_params=_cparams(("parallel", "parallel", "arbitrary")),
        name="ssd_conv_silu",
    )(zxbc, zxbc, zxbc, shift, conv_w, conv_b)


def _bf16_split(v, parts):
    pieces = []
    for _ in range(parts):
        p = v.astype(BF16)
        pieces.append(p)
        v = v - p.astype(F32)
    return jnp.concatenate(pieces, axis=1)


def _scan_kernel(*refs, reverse, final):
    if final:
        (x_ref, b_ref, c_ref, dt_ref, cs_ref, sel_ref,
         yf_ref, z_ref, dskip_ref, ng_ref, o_ref, state_ref, cst_ref) = refs
    else:
        (x_ref, b_ref, c_ref, dt_ref, cs_ref, sel_ref, o_ref, state_ref, cst_ref) = refs
    g = pl.program_id(1)
    j = pl.program_id(2)
    direction = 1 if reverse else 0
    head_pairs = HEADS_PER_GROUP // 2
    half = LANES // 2

    @pl.when(j == 0)
    def _():
        state_ref[...] = jnp.zeros_like(state_ref)

    sel = sel_ref[...]
    li = lax.broadcasted_iota(jnp.int32, (CHUNK, CHUNK), 0)
    si = lax.broadcasted_iota(jnp.int32, (CHUNK, CHUNK), 1)
    mask = (li <= si) if reverse else (li >= si)
    low_lanes = lax.broadcasted_iota(jnp.int32, (1, LANES), 1) < half
    end_row = 0 if reverse else CHUNK - 1
    row0 = pl.multiple_of(direction * SSD_HEADS + g * HEADS_PER_GROUP, HEADS_PER_GROUP)

    prev = state_ref[...]
    order = range(SCAN_CHUNKS - 1, -1, -1) if reverse else range(SCAN_CHUNKS)
    for u in order:
        rows = slice(u * CHUNK, (u + 1) * CHUNK)
        x = x_ref[rows, :].astype(F32)
        bm = b_ref[rows, :]
        cm = c_ref[rows, :]
        cs = cs_ref[rows, :]

        cs64 = jnp.dot(_bf16_split(cs, 3), sel, preferred_element_type=F32)
        dt64 = jnp.dot(_bf16_split(dt_ref[rows, :], 2), sel[:2 * N_DT, :], preferred_element_type=F32)
        cs_end64 = cs64[end_row:end_row + 1, :]

        cst_ref[u] = cs.T
        cs_rows = cst_ref[u, pl.ds(row0, HEADS_PER_GROUP), :]

        cb = lax.dot_general(cm, bm, (((1,), (1,)), ((), ())), preferred_element_type=F32)
        xdt = x * dt64
        xdt_b = xdt.astype(BF16)
        zero_b = jnp.zeros((CHUNK, LANES), BF16)

        y_parts = []
        for pr in range(head_pairs):
            cols = slice(pr * LANES, (pr + 1) * LANES)
            v = cs64[:, cols]
            r = pltpu.roll(v, half, axis=1)
            m_pair = []
            for hh, cs_col in ((2 * pr, jnp.where(low_lanes, v, r)), (2 * pr + 1, jnp.where(low_lanes, r, v))):
                seg = cs_col - cs_rows[hh:hh + 1, :]
                decay = jnp.exp2(jnp.where(mask, seg, -jnp.inf))
                m_pair.append((cb * decay).astype(BF16))
            xp = xdt_b[:, cols]
            x_pair = jnp.concatenate([jnp.where(low_lanes, xp, zero_b), jnp.where(low_lanes, zero_b, xp)], axis=0)
            y_parts.append(jnp.dot(jnp.concatenate(m_pair, axis=1), x_pair, preferred_element_type=F32))
        y = jnp.concatenate(y_parts, axis=1)

        y = y + jnp.dot(cm, prev.astype(BF16), preferred_element_type=F32) * jnp.exp2(cs64)

        xw = (xdt * jnp.exp2(cs_end64 - cs64)).astype(BF16)
        new_states = lax.dot_general(bm, xw, (((0,), (0,)), ((), ())), preferred_element_type=F32)
        prev = jnp.exp2(cs_end64) * prev + new_states

        if final:
            y = y + yf_ref[rows, :] + dskip_ref[...] * x
            z = z_ref[rows, :].astype(F32)
            y = y * (z * (1.0 / (1.0 + jnp.exp(-z))))
            o_ref[rows, :] = (y * _rms_scale(y) * ng_ref[...]).astype(BF16)
        else:
            o_ref[rows, :] = y
    state_ref[...] = prev


def _ssd_scan(xbc, dt, cs, sel, seq_len, final_inputs=None):
    t = xbc.shape[0]
    n_seq = t // seq_len
    rows = SCAN_CHUNKS * CHUNK
    steps = seq_len // rows
    reverse = final_inputs is not None
    b_col = D_INNER // D_STATE
    c_col = b_col + SSD_GROUPS

    def row(s, j):
        return s * steps + ((steps - 1 - j) if reverse else j)

    direction = 1 if reverse else 0
    in_specs = [
        pl.BlockSpec((rows, GROUP_WIDTH), lambda s, g, j: (row(s, j), g)),
        pl.BlockSpec((rows, D_STATE), lambda s, g, j: (row(s, j), b_col + g)),
        pl.BlockSpec((rows, D_STATE), lambda s, g, j: (row(s, j), c_col + g)),
        pl.BlockSpec((rows, N_DT), lambda s, g, j: (row(s, j), 0)),
        pl.BlockSpec((rows, N_DT), lambda s, g, j: (row(s, j), 0)),
        pl.BlockSpec((None,) + sel.shape[1:], lambda s, g, j: (direction * SSD_GROUPS + g, 0, 0)),
    ]
    args = [xbc, xbc, xbc, dt, cs, sel]
    if reverse:
        y_fwd, zxbc, d_skip, norm_g = final_inputs
        in_specs += [
            pl.BlockSpec((rows, GROUP_WIDTH), lambda s, g, j: (row(s, j), g)),
            pl.BlockSpec((rows, GROUP_WIDTH), lambda s, g, j: (row(s, j), g)),
            pl.BlockSpec((1, GROUP_WIDTH), lambda s, g, j: (0, g)),
            pl.BlockSpec((1, GROUP_WIDTH), lambda s, g, j: (0, g)),
        ]
        args += [y_fwd, zxbc, d_skip, norm_g]
    return pl.pallas_call(
        functools.partial(_scan_kernel, reverse=reverse, final=reverse),
        grid=(n_seq, SSD_GROUPS, steps),
        in_specs=in_specs,
        out_specs=pl.BlockSpec((rows, GROUP_WIDTH), lambda s, g, j: (row(s, j), g)),
        out_shape=jax.ShapeDtypeStruct((t, D_INNER), BF16 if reverse else F32),
        scratch_shapes=[pltpu.VMEM((D_STATE, GROUP_WIDTH), F32),
                        pltpu.VMEM((SCAN_CHUNKS, N_DT, CHUNK), F32)],
        compiler_params=_cparams(("parallel", "parallel", "arbitrary")),
        name="ssd_scan_bwd" if reverse else "ssd_scan_fwd",
    )(*args)


def _rope_tables(seq_len):
    rows = seq_len // GRID_W
    row = jnp.repeat(jnp.arange(rows, dtype=F32), GRID_W)
    col = jnp.tile(jnp.arange(GRID_W, dtype=F32), rows)
    inv_freq = ROPE_THETA ** (-jnp.arange(0, ROPE_AXIS_DIM, 2, dtype=F32) / ROPE_AXIS_DIM)
    ang = jnp.stack([row, col], axis=-1)[..., None] * inv_freq
    cos = jnp.cos(ang).reshape(seq_len, ROPE_AXIS_DIM)
    sin = jnp.sin(ang).reshape(seq_len, ROPE_AXIS_DIM)
    rc = jnp.concatenate([cos, cos], axis=1)
    rs = jnp.concatenate([-sin, sin], axis=1)
    return rc, rs


def _permute_rope_heads(a):
    lead = a.shape[:-1]
    a = a.reshape(lead + (a.shape[-1] // HEAD_DIM, 2, 2, ROPE_AXIS_DIM // 2))
    return jnp.swapaxes(a, -2, -3).reshape(lead + (-1,))


def _head_selectors():
    col = jnp.arange(N_DT)[None, :, None]
    dg = jnp.arange(2 * SSD_GROUPS)[:, None, None]
    first = (dg // SSD_GROUPS) * SSD_HEADS + (dg % SSD_GROUPS) * HEADS_PER_GROUP
    lane_head = jnp.arange(GROUP_WIDTH)[None, None, :] // SSD_HEAD_DIM
    e = (col == first + lane_head).astype(BF16)
    return jnp.concatenate([e, e, e], axis=1)


def kernel(x_prompt, x_sample, norm_mix, norm_mlp, attn_w_qkv, attn_q_norm, attn_k_norm, attn_w_o,
           ssd_w_in, ssd_conv_w, ssd_conv_b, ssd_dt_bias, ssd_a_log, ssd_d, ssd_norm, ssd_w_out,
           mlp_w_up, mlp_w_down):
    seq_len = x_prompt.shape[1]
    assert x_sample.shape[1] == seq_len and x_prompt.shape[2] == D_MODEL
    n_prompt = x_prompt.shape[0] * seq_len
    h = jnp.concatenate([x_prompt.reshape(-1, D_MODEL), x_sample.reshape(-1, D_MODEL)], axis=0)
    depth = norm_mix.shape[0]

    rc, rs = _rope_tables(seq_len)
    sel = _head_selectors()

    for i in range(depth):
        jdx = i // 2
        g_mix = norm_mix[i].reshape(1, D_MODEL)
        if i % 2 == 0:
            gains = _permute_rope_heads(jnp.stack([attn_q_norm[jdx] * QK_LOG2_SCALE, attn_k_norm[jdx]], axis=0))
            w_qkv = attn_w_qkv[jdx].astype(BF16)
            w_qkv = jnp.concatenate([_permute_rope_heads(w_qkv[:, :D_Q + D_KV]), w_qkv[:, D_Q + D_KV:]], axis=1)
            qkv = _qkv_proj(h, g_mix, w_qkv, gains, rc, rs, seq_len)
            o = _flash_attention(qkv, attn_q_norm[jdx], attn_k_norm[jdx], seq_len)
            h = _proj_residual(o, attn_w_o[jdx].astype(BF16), h)
        else:
            w_in = ssd_w_in[jdx]
            zxbc, dt, cs = _in_proj(h, g_mix, w_in[:, :D_ZXBC].astype(BF16),
                                    w_in[:, D_ZXBC:].astype(BF16),
                                    ssd_dt_bias[jdx].reshape(1, N_DT), ssd_a_log[jdx].reshape(1, N_DT))
            xbc = _conv_silu(zxbc, ssd_conv_w[jdx], ssd_conv_b[jdx].reshape(1, CONV_DIM), seq_len)
            y_fwd = _ssd_scan(xbc, dt, cs, sel, seq_len)
            d_skip = jnp.repeat(ssd_d[jdx], SSD_HEAD_DIM).reshape(1, D_INNER)
            y = _ssd_scan(xbc, dt, cs, sel, seq_len,
                          final_inputs=(y_fwd, zxbc, d_skip, ssd_norm[jdx].reshape(1, D_INNER)))
            h = _proj_residual(y, ssd_w_out[jdx].astype(BF16), h)
        h = _mlp(h, norm_mlp[i].reshape(1, D_MODEL), mlp_w_up[i].astype(BF16),
                 mlp_w_down[i].astype(BF16))

    y_prompt = h[:n_prompt].reshape(x_prompt.shape)
    y_sample = h[n_prompt:].reshape(x_sample.shape)
    return (y_prompt, y_sample)
```

```python
import functools
import math

import jax
import jax.numpy as jnp
from jax import lax
from jax.experimental import pallas as pl
from jax.experimental.pallas import tpu as pltpu

F32 = jnp.float32
BF16 = jnp.bfloat16

D_MODEL = 2048
GRID_W = 64
HEAD_DIM = 128
N_HEADS = D_MODEL // HEAD_DIM
N_KV_HEADS = N_HEADS // 2
GQ = N_HEADS // N_KV_HEADS
D_Q = N_HEADS * HEAD_DIM
D_KV = N_KV_HEADS * HEAD_DIM
D_QKV = D_Q + 2 * D_KV
ROPE_AXIS_DIM = HEAD_DIM // 2
ROPE_THETA = 10000.0
D_INNER = 2 * D_MODEL
SSD_HEAD_DIM = 64
SSD_HEADS = D_INNER // SSD_HEAD_DIM
SSD_GROUPS = 8
HEADS_PER_GROUP = SSD_HEADS // SSD_GROUPS
GROUP_WIDTH = D_INNER // SSD_GROUPS
D_STATE = 128
D_CONV = 5
CONV_PAD = (D_CONV - 1) // 2
CHUNK = 128
CONV_DIM = D_INNER + 2 * SSD_GROUPS * D_STATE
D_ZXBC = D_INNER + CONV_DIM
N_DT = 2 * SSD_HEADS
D_FF = 4 * D_MODEL
NORM_EPS = 1e-6

LANES = 128
BF16_SUBLANES = 16
VMEM_LIMIT = 56 * 1024 * 1024

TM = 512
TM_MLP = 512
TM_IN = 1024
TM_OUT = 1024
TN_QKV = 512
TN_IN = 2048
TN_OUT = 1024
TF = 1024
TQ = 1024
TK = 256
TKV = 1024
SCAN_CHUNKS = 8
STATE_CHUNKS = 16
CONV_ROWS = 1024
CONV_COLS = 2048


def _cparams(sem):
    return pltpu.CompilerParams(dimension_semantics=sem, vmem_limit_bytes=VMEM_LIMIT)


def _rms_scale(x):
    return lax.rsqrt(jnp.mean(x * x, axis=-1, keepdims=True) + NORM_EPS)


def _qkv_kernel(h_ref, g_ref, w_ref, gain_ref, rc_ref, rs_ref, o_ref):
    x = h_ref[...]
    xn = (x * _rms_scale(x) * g_ref[...]).astype(BF16)
    rc = rc_ref[...]
    rs = rs_ref[...]
    for c in range(D_QKV // TN_QKV):
        cols = slice(c * TN_QKV, (c + 1) * TN_QKV)
        acc = jnp.dot(xn, w_ref[:, cols], preferred_element_type=F32)
        if c * TN_QKV >= D_Q + D_KV:
            o_ref[:, cols] = acc.astype(BF16)
            continue
        gain = gain_ref[0:1, :] if c * TN_QKV < D_Q else gain_ref[1:2, :]
        for hh in range(TN_QKV // HEAD_DIM):
            xh = acc[:, hh * HEAD_DIM:(hh + 1) * HEAD_DIM]
            y = xh * _rms_scale(xh) * gain
            rot = pltpu.roll(y, HEAD_DIM // 2, axis=1)
            o_ref[:, c * TN_QKV + hh * HEAD_DIM:c * TN_QKV + (hh + 1) * HEAD_DIM] = (
                y * rc + rot * rs).astype(BF16)


def _qkv_proj(h, g, w, gains, rc, rs, seq_len):
    t = h.shape[0]
    pos_tiles = seq_len // TM
    pos_map = lambda i: (i % pos_tiles, 0)
    return pl.pallas_call(
        _qkv_kernel,
        grid=(t // TM,),
        in_specs=[
            pl.BlockSpec((TM, D_MODEL), lambda i: (i, 0)),
            pl.BlockSpec((1, D_MODEL), lambda i: (0, 0)),
            pl.BlockSpec((D_MODEL, D_QKV), lambda i: (0, 0), pipeline_mode=pl.Buffered(1)),
            pl.BlockSpec((2, HEAD_DIM), lambda i: (0, 0)),
            pl.BlockSpec((TM, HEAD_DIM), pos_map),
            pl.BlockSpec((TM, HEAD_DIM), pos_map),
        ],
        out_specs=pl.BlockSpec((TM, D_QKV), lambda i: (i, 0)),
        out_shape=jax.ShapeDtypeStruct((t, D_QKV), BF16),
        compiler_params=_cparams(("parallel",)),
        name="qkv_proj",
    )(h, g, w, gains, rc, rs)


def _flash_kernel(q_ref, k_ref, v_ref, o_ref, vt_ref, *, seq_len, unshifted):
    qi = pl.program_id(2)
    n_groups = seq_len // TKV
    nq = GQ * TQ
    sub = 8

    @pl.when(qi == 0)
    def _():
        def fill(i, carry):
            start = pl.multiple_of(i * TKV, TKV)
            vt_ref[i] = v_ref[pl.ds(start, TKV), :].astype(F32).T.astype(BF16)
            return carry
        lax.fori_loop(0, n_groups, fill, 0)

    qt = jnp.concatenate([q_ref[:, g * HEAD_DIM:(g + 1) * HEAD_DIM].astype(F32).T.astype(BF16)
                          for g in range(GQ)], axis=1)

    if unshifted:
        def scores(t):
            return jnp.dot(k_ref[t * TK:(t + 1) * TK, :], qt, preferred_element_type=F32)

        tiles_per_group = TKV // TK
        n_kv = seq_len // TK
        l = jnp.zeros((sub, nq), F32)
        acc = jnp.zeros((HEAD_DIM, nq), F32)
        st = scores(0)
        p_group = []
        for t in range(n_kv):
            st_next = scores(t + 1) if t + 1 < n_kv else None
            p = jnp.exp2(st)
            l = l + jnp.sum(p.reshape(TK // sub, sub, nq), axis=0)
            p_group.append(p.astype(BF16))
            if len(p_group) == tiles_per_group:
                acc = acc + jnp.dot(vt_ref[t // tiles_per_group], jnp.concatenate(p_group, axis=0),
                                    preferred_element_type=F32)
                p_group = []
            st = st_next
        l = jnp.sum(l, axis=0, keepdims=True)
    else:
        def body(i, carry):
            m, l, acc = carry
            start = pl.multiple_of(i * TKV, TKV)
            sc = jnp.dot(k_ref[pl.ds(start, TKV), :], qt, preferred_element_type=F32)
            m_new = jnp.maximum(m, jnp.max(sc, axis=0, keepdims=True))
            alpha = jnp.exp2(m - m_new)
            p = jnp.exp2(sc - m_new)
            l = alpha * l + jnp.sum(p, axis=0, keepdims=True)
            acc = alpha * acc + jnp.dot(vt_ref[i], p.astype(BF16), preferred_element_type=F32)
            return m_new, l, acc

        init = (jnp.full((1, nq), -jnp.inf, F32), jnp.zeros((1, nq), F32),
                jnp.zeros((HEAD_DIM, nq), F32))
        _, l, acc = lax.fori_loop(0, n_groups, body, init)
    out_t = acc / l
    for g in range(GQ):
        o_ref[:, g * HEAD_DIM:(g + 1) * HEAD_DIM] = out_t[:, g * TQ:(g + 1) * TQ].T.astype(BF16)


def _flash_call(qkv, seq_len, unshifted):
    assert seq_len % TKV == 0 and TKV % TK == 0 and seq_len % TQ == 0
    t = qkv.shape[0]
    n_seq = t // seq_len
    q_tiles = seq_len // TQ
    k_col = D_Q // HEAD_DIM
    v_col = (D_Q + D_KV) // HEAD_DIM
    return pl.pallas_call(
        functools.partial(_flash_kernel, seq_len=seq_len, unshifted=unshifted),
        grid=(n_seq, N_KV_HEADS, q_tiles),
        in_specs=[
            pl.BlockSpec((TQ, GQ * HEAD_DIM), lambda s, kv, qi: (s * q_tiles + qi, kv)),
            pl.BlockSpec((seq_len, HEAD_DIM), lambda s, kv, qi: (s, k_col + kv)),
            pl.BlockSpec((seq_len, HEAD_DIM), lambda s, kv, qi: (s, v_col + kv)),
        ],
        out_specs=pl.BlockSpec((TQ, GQ * HEAD_DIM), lambda s, kv, qi: (s * q_tiles + qi, kv)),
        out_shape=jax.ShapeDtypeStruct((t, D_Q), BF16),
        scratch_shapes=[pltpu.VMEM((seq_len // TKV, HEAD_DIM, TKV), BF16)],
        compiler_params=_cparams(("parallel", "parallel", "arbitrary")),
        name="flash_unshifted" if unshifted else "flash_online",
    )(qkv, qkv, qkv)


FLASH_MAX_BOUND = 60.0
QK_LOG2_SCALE = (HEAD_DIM ** -0.5) * math.log2(math.e)


def _flash_attention(qkv, q_gain, k_gain, seq_len):
    bound = (QK_LOG2_SCALE * HEAD_DIM * (1.0 + 2.0 ** -7)) * (
        jnp.max(jnp.abs(q_gain)) * jnp.max(jnp.abs(k_gain)))
    return lax.cond(bound <= FLASH_MAX_BOUND,
                    lambda: _flash_call(qkv, seq_len, True),
                    lambda: _flash_call(qkv, seq_len, False))


def _proj_residual_kernel(a_ref, w_ref, h_ref, o_ref):
    o_ref[...] = h_ref[...] + jnp.dot(a_ref[...], w_ref[...], preferred_element_type=F32)


def _proj_residual(a, w, h):
    t, k = a.shape
    return pl.pallas_call(
        _proj_residual_kernel,
        grid=(t // TM_OUT, D_MODEL // TN_OUT),
        in_specs=[
            pl.BlockSpec((TM_OUT, k), lambda i, j: (i, 0)),
            pl.BlockSpec((k, TN_OUT), lambda i, j: (0, j)),
            pl.BlockSpec((TM_OUT, TN_OUT), lambda i, j: (i, j)),
        ],
        out_specs=pl.BlockSpec((TM_OUT, TN_OUT), lambda i, j: (i, j)),
        out_shape=jax.ShapeDtypeStruct((t, D_MODEL), F32),
        input_output_aliases={2: 0},
        compiler_params=_cparams(("parallel", "arbitrary")),
        name="proj_residual",
    )(a, w, h)


def _mlp_kernel(h_ref, g_ref, wu_ref, wd_ref, o_ref, xn_ref):
    f = pl.program_id(1)

    @pl.when(f == 0)
    def _():
        x = h_ref[...]
        xn_ref[...] = (x * _rms_scale(x) * g_ref[...]).astype(BF16)
        o_ref[...] = x

    u = jnp.maximum(jnp.dot(xn_ref[...], wu_ref[...], preferred_element_type=F32), 0.0)
    o_ref[...] += jnp.dot((u * u).astype(BF16), wd_ref[...], preferred_element_type=F32)


def _mlp(h, g, w_up, w_down, row_start=0, n_rows=None):
    n_rows = h.shape[0] if n_rows is None else n_rows
    assert row_start % TM_MLP == 0 and n_rows % TM_MLP == 0
    first_tile = row_start // TM_MLP
    return pl.pallas_call(
        _mlp_kernel,
        grid=(n_rows // TM_MLP, D_FF // TF),
        in_specs=[
            pl.BlockSpec((TM_MLP, D_MODEL), lambda i, f: (first_tile + i, 0)),
            pl.BlockSpec((1, D_MODEL), lambda i, f: (0, 0)),
            pl.BlockSpec((D_MODEL, TF), lambda i, f: (0, f)),
            pl.BlockSpec((TF, D_MODEL), lambda i, f: (f, 0)),
        ],
        out_specs=pl.BlockSpec((TM_MLP, D_MODEL), lambda i, f: (i, 0)),
        out_shape=jax.ShapeDtypeStruct((n_rows, D_MODEL), F32),
        scratch_shapes=[pltpu.VMEM((TM_MLP, D_MODEL), BF16)],
        compiler_params=_cparams(("parallel", "arbitrary")),
        name="mlp",
    )(h, g, w_up, w_down)


def _chunk_cumsum(v, reverse):
    n = v.shape[0]
    pos = lax.broadcasted_iota(jnp.int32, v.shape, 0) % CHUNK
    k = 1
    while k < CHUNK:
        if reverse:
            v = v + jnp.where(pos < CHUNK - k, pltpu.roll(v, n - k, axis=0), 0.0)
        else:
            v = v + jnp.where(pos >= k, pltpu.roll(v, k, axis=0), 0.0)
        k *= 2
    return v


def _in_proj_kernel(h_ref, g_ref, w_ref, wdt_ref, bias_ref, alog_ref, o_ref, dt_ref, cs_ref, xn_ref):
    j = pl.program_id(1)

    @pl.when(j == 0)
    def _():
        x = h_ref[...]
        xn_ref[...] = (x * _rms_scale(x) * g_ref[...]).astype(BF16)
        dtr = jnp.dot(xn_ref[...], wdt_ref[...], preferred_element_type=F32) + bias_ref[...]
        dt = jnp.maximum(dtr, 0.0) + jnp.log1p(jnp.exp(-jnp.abs(dtr)))
        da = dt * (-jnp.exp(alog_ref[...]) * math.log2(math.e))
        backward_cols = lax.broadcasted_iota(jnp.int32, da.shape, 1) >= SSD_HEADS
        dt_ref[...] = dt
        cs_ref[...] = jnp.where(backward_cols, _chunk_cumsum(da, True), _chunk_cumsum(da, False))

    o_ref[...] = jnp.dot(xn_ref[...], w_ref[...], preferred_element_type=F32).astype(BF16)


def _in_proj(h, g, w_main, w_dt, dt_bias, a_log):
    t = h.shape[0]
    row = lambda i, j: (0, 0)
    return pl.pallas_call(
        _in_proj_kernel,
        grid=(t // TM_IN, D_ZXBC // TN_IN),
        in_specs=[
            pl.BlockSpec((TM_IN, D_MODEL), lambda i, j: (i, 0)),
            pl.BlockSpec((1, D_MODEL), row),
            pl.BlockSpec((D_MODEL, TN_IN), lambda i, j: (0, j)),
            pl.BlockSpec((D_MODEL, N_DT), row),
            pl.BlockSpec((1, N_DT), row),
            pl.BlockSpec((1, N_DT), row),
        ],
        out_specs=[
            pl.BlockSpec((TM_IN, TN_IN), lambda i, j: (i, j)),
            pl.BlockSpec((TM_IN, N_DT), lambda i, j: (i, 0)),
            pl.BlockSpec((TM_IN, N_DT), lambda i, j: (i, 0)),
        ],
        out_shape=[jax.ShapeDtypeStruct((t, D_ZXBC), BF16),
                   jax.ShapeDtypeStruct((t, N_DT), F32),
                   jax.ShapeDtypeStruct((t, N_DT), F32)],
        scratch_shapes=[pltpu.VMEM((TM_IN, D_MODEL), BF16)],
        compiler_params=_cparams(("parallel", "arbitrary")),
        name="ssd_in_proj",
    )(h, g, w_main, w_dt, dt_bias, a_log)


def _conv_kernel(prev_ref, cur_ref, next_ref, shift_ref, w_ref, b_ref, o_ref, ext_ref, *, row_tiles):
    r = pl.program_id(1)
    halo = BF16_SUBLANES
    zero = jnp.zeros((halo, CONV_COLS), BF16)
    ext_ref[0:halo, :] = jnp.where(r == 0, zero, prev_ref[...])
    ext_ref[halo:halo + CONV_ROWS, :] = cur_ref[...]
    ext_ref[halo + CONV_ROWS:, :] = jnp.where(r == row_tiles - 1, zero, next_ref[...])
    side_taps = [k for k in range(D_CONV) if k != CONV_PAD]
    for blk in range(CONV_ROWS // CHUNK):
        lo = blk * CHUNK
        window = ext_ref[lo:lo + CHUNK + 2 * halo, :]
        shifted = jnp.dot(shift_ref[...], window, preferred_element_type=F32)
        acc = b_ref[...] + ext_ref[lo + halo:lo + halo + CHUNK, :].astype(F32) * w_ref[CONV_PAD:CONV_PAD + 1, :]
        for i, k in enumerate(side_taps):
            acc = acc + shifted[i * CHUNK:(i + 1) * CHUNK, :] * w_ref[k:k + 1, :]
        o_ref[lo:lo + CHUNK, :] = (acc * (1.0 / (1.0 + jnp.exp(-acc)))).astype(BF16)


def _conv_shift_matrix():
    halo = BF16_SUBLANES
    t = jnp.arange(CHUNK)[:, None]
    j = jnp.arange(CHUNK + 2 * halo)[None, :]
    blocks = [(j == t + halo + (k - CONV_PAD)) for k in range(D_CONV) if k != CONV_PAD]
    return jnp.concatenate(blocks, axis=0).astype(BF16)


def _conv_silu(zxbc, conv_w, conv_b, seq_len):
    t = zxbc.shape[0]
    n_seq = t // seq_len
    row_tiles = seq_len // CONV_ROWS
    col0 = D_INNER // CONV_COLS
    halo_per_tile = CONV_ROWS // BF16_SUBLANES
    last_halo = t // BF16_SUBLANES - 1
    shift = _conv_shift_matrix()

    def cur_map(s, r, c):
        return (s * row_tiles + r, col0 + c)

    def prev_map(s, r, c):
        return (jnp.maximum((s * row_tiles + r) * halo_per_tile - 1, 0), col0 + c)

    def next_map(s, r, c):
        return (jnp.minimum((s * row_tiles + r + 1) * halo_per_tile, last_halo), col0 + c)

    return pl.pallas_call(
        functools.partial(_conv_kernel, row_tiles=row_tiles),
        grid=(n_seq, row_tiles, CONV_DIM // CONV_COLS),
        in_specs=[
            pl.BlockSpec((BF16_SUBLANES, CONV_COLS), prev_map),
            pl.BlockSpec((CONV_ROWS, CONV_COLS), cur_map),
            pl.BlockSpec((BF16_SUBLANES, CONV_COLS), next_map),
            pl.BlockSpec(shift.shape, lambda s, r, c: (0, 0)),
            pl.BlockSpec((D_CONV, CONV_COLS), lambda s, r, c: (0, c)),
            pl.BlockSpec((1, CONV_COLS), lambda s, r, c: (0, c)),
        ],
        out_specs=pl.BlockSpec((CONV_ROWS, CONV_COLS), lambda s, r, c: (s * row_tiles + r, c)),
        out_shape=jax.ShapeDtypeStruct((t, CONV_DIM), BF16),
        scratch_shapes=[pltpu.VMEM((CONV_ROWS + 2 * BF16_SUBLANES, CONV_COLS), BF16)],
        compiler_params=_cparams(("parallel", "parallel", "arbitrary")),
        name="ssd_conv_silu",
    )(zxbc, zxbc, zxbc, shift, conv_w, conv_b)


def _bf16_split(v, parts):
    pieces = []
    for _ in range(parts):
        p = v.astype(BF16)
        pieces.append(p)
        v = v - p.astype(F32)
    return jnp.concatenate(pieces, axis=1)


def _head_broadcast(cs, dt, sel):
    cs_wide = jnp.dot(_bf16_split(cs, 3), sel, preferred_element_type=F32)
    dt_wide = jnp.dot(_bf16_split(dt, 2), sel[:2 * N_DT, :], preferred_element_type=F32)
    return cs_wide, dt_wide


def _states_kernel(xf_ref, bf_ref, csf_ref, dtf_ref, self_ref, xb_ref, bb_ref, csb_ref, dtb_ref, selb_ref,
                   pf_ref, pb_ref, sf_ref, sb_ref):
    j = pl.program_id(2)

    @pl.when(j == 0)
    def _():
        sf_ref[...] = jnp.zeros_like(sf_ref)
        sb_ref[...] = jnp.zeros_like(sb_ref)

    directions = ((False, xf_ref, bf_ref, csf_ref, dtf_ref, self_ref, pf_ref, sf_ref),
                  (True, xb_ref, bb_ref, csb_ref, dtb_ref, selb_ref, pb_ref, sb_ref))
    chunk_rows = [slice(u * CHUNK, (u + 1) * CHUNK) for u in range(STATE_CHUNKS)]

    weighted = {}
    for reverse, x_ref, b_ref, cs_ref, dt_ref, sel_ref, p_ref, s_ref in directions:
        sel = sel_ref[...]
        end_row = 0 if reverse else CHUNK - 1
        for u, rows in enumerate(chunk_rows):
            cs = cs_ref[rows, :]
            cs_end = cs[end_row:end_row + 1, :]
            w = dt_ref[rows, :] * jnp.exp2(jnp.minimum(cs_end - cs, 0.0))
            narrow = jnp.concatenate([w, jnp.broadcast_to(jnp.exp2(cs_end), (8, N_DT))], axis=0)
            wide = jnp.dot(_bf16_split(narrow, 2), sel[:2 * N_DT, :], preferred_element_type=F32)
            xw = (x_ref[rows, :].astype(F32) * wide[:CHUNK, :]).astype(BF16)
            weighted[reverse, u] = (xw, wide[CHUNK:CHUNK + 1, :])
    contrib = {}
    for reverse, x_ref, b_ref, cs_ref, dt_ref, sel_ref, p_ref, s_ref in directions:
        for u, rows in enumerate(chunk_rows):
            contrib[reverse, u] = lax.dot_general(b_ref[rows, :], weighted[reverse, u][0],
                                                  (((0,), (0,)), ((), ())), preferred_element_type=F32)
    for reverse, x_ref, b_ref, cs_ref, dt_ref, sel_ref, p_ref, s_ref in directions:
        prev = s_ref[...]
        for u in (range(STATE_CHUNKS - 1, -1, -1) if reverse else range(STATE_CHUNKS)):
            p_ref[u] = prev.astype(BF16)
            prev = weighted[reverse, u][1] * prev + contrib[reverse, u]
        s_ref[...] = prev


def _ssd_states(xbc, dt, cs, sel, seq_len):
    t = xbc.shape[0]
    n_seq = t // seq_len
    rows = STATE_CHUNKS * CHUNK
    steps = seq_len // rows
    b_col = D_INNER // D_STATE

    def fwd(s, j):
        return s * steps + j

    def bwd(s, j):
        return s * steps + (steps - 1 - j)

    def specs(row, direction):
        return [
            pl.BlockSpec((rows, GROUP_WIDTH), lambda s, g, j: (row(s, j), g)),
            pl.BlockSpec((rows, D_STATE), lambda s, g, j: (row(s, j), b_col + g)),
            pl.BlockSpec((rows, N_DT), lambda s, g, j: (row(s, j), 0)),
            pl.BlockSpec((rows, N_DT), lambda s, g, j: (row(s, j), 0)),
            pl.BlockSpec((None,) + sel.shape[1:], lambda s, g, j: (direction * SSD_GROUPS + g, 0, 0)),
        ]

    state_shape = jax.ShapeDtypeStruct((t // CHUNK, D_STATE, D_INNER), BF16)
    return pl.pallas_call(
        _states_kernel,
        grid=(n_seq, SSD_GROUPS, steps),
        in_specs=specs(fwd, 0) + specs(bwd, 1),
        out_specs=[
            pl.BlockSpec((STATE_CHUNKS, D_STATE, GROUP_WIDTH), lambda s, g, j: (fwd(s, j), 0, g)),
            pl.BlockSpec((STATE_CHUNKS, D_STATE, GROUP_WIDTH), lambda s, g, j: (bwd(s, j), 0, g)),
        ],
        out_shape=[state_shape, state_shape],
        scratch_shapes=[pltpu.VMEM((D_STATE, GROUP_WIDTH), F32), pltpu.VMEM((D_STATE, GROUP_WIDTH), F32)],
        compiler_params=_cparams(("parallel", "parallel", "arbitrary")),
        name="ssd_states",
    )(xbc, xbc, cs, dt, sel, xbc, xbc, cs, dt, sel)


def _ssd_out_kernel(x_ref, b_ref, c_ref, cs_ref, dt_ref, sel_ref, pf_ref, pb_ref, z_ref, dskip_ref, ng_ref,
                    o_ref, cst_ref):
    g = pl.program_id(1)
    head_pairs = HEADS_PER_GROUP // 2
    half = LANES // 2
    sel = sel_ref[...]
    li = lax.broadcasted_iota(jnp.int32, (CHUNK, CHUNK), 0)
    si = lax.broadcasted_iota(jnp.int32, (CHUNK, CHUNK), 1)
    low_lanes = lax.broadcasted_iota(jnp.int32, (1, LANES), 1) < half
    zero_b = jnp.zeros((CHUNK, LANES), BF16)

    def stage(u):
        rows = slice(u * CHUNK, (u + 1) * CHUNK)
        cs = cs_ref[rows, :]
        cs_both, dt_both = _head_broadcast(cs, dt_ref[rows, :], sel)
        cst_ref[u] = cs.T
        cb = lax.dot_general(c_ref[rows, :], b_ref[rows, :], (((1,), (1,)), ((), ())),
                             preferred_element_type=F32)
        return cs_both, dt_both, cb

    staged = [stage(u) for u in range(SCAN_CHUNKS)]
    for u in range(SCAN_CHUNKS):
        rows = slice(u * CHUNK, (u + 1) * CHUNK)
        cs_both, dt_both, cb = staged[u]
        x = x_ref[rows, :].astype(F32)
        cm = c_ref[rows, :]

        y = dskip_ref[...] * x
        for direction, p_ref in ((0, pf_ref), (1, pb_ref)):
            wide = slice(direction * GROUP_WIDTH, (direction + 1) * GROUP_WIDTH)
            cs64 = cs_both[:, wide]
            xdt_b = (x * dt_both[:, wide]).astype(BF16)
            row0 = pl.multiple_of(direction * SSD_HEADS + g * HEADS_PER_GROUP, HEADS_PER_GROUP)
            cs_rows = cst_ref[u, pl.ds(row0, HEADS_PER_GROUP), :]
            mask = (li <= si) if direction else (li >= si)

            y_parts = []
            for pr in range(head_pairs):
                cols = slice(pr * LANES, (pr + 1) * LANES)
                v = cs64[:, cols]
                r = pltpu.roll(v, half, axis=1)
                m_pair = []
                for hh, cs_col in ((2 * pr, jnp.where(low_lanes, v, r)), (2 * pr + 1, jnp.where(low_lanes, r, v))):
                    seg = cs_col - cs_rows[hh:hh + 1, :]
                    decay = jnp.exp2(jnp.where(mask, seg, -jnp.inf))
                    m_pair.append((cb * decay).astype(BF16))
                xp = xdt_b[:, cols]
                x_pair = jnp.concatenate([jnp.where(low_lanes, xp, zero_b), jnp.where(low_lanes, zero_b, xp)],
                                         axis=0)
                y_parts.append(jnp.dot(jnp.concatenate(m_pair, axis=1), x_pair, preferred_element_type=F32))
            y = y + jnp.concatenate(y_parts, axis=1)
            y = y + jnp.dot(cm, p_ref[u], preferred_element_type=F32) * jnp.exp2(cs64)

        z = z_ref[rows, :].astype(F32)
        y = y * (z * (1.0 / (1.0 + jnp.exp(-z))))
        o_ref[rows, :] = (y * _rms_scale(y) * ng_ref[...]).astype(BF16)


def _ssd_out(xbc, zxbc, dt, cs, sel_both, prev_f, prev_b, d_skip, norm_g):
    t = xbc.shape[0]
    rows = SCAN_CHUNKS * CHUNK
    b_col = D_INNER // D_STATE
    c_col = b_col + SSD_GROUPS
    tile = lambda i, g: (i, g)
    return pl.pallas_call(
        _ssd_out_kernel,
        grid=(t // rows, SSD_GROUPS),
        in_specs=[
            pl.BlockSpec((rows, GROUP_WIDTH), tile),
            pl.BlockSpec((rows, D_STATE), lambda i, g: (i, b_col + g)),
            pl.BlockSpec((rows, D_STATE), lambda i, g: (i, c_col + g)),
            pl.BlockSpec((rows, N_DT), lambda i, g: (i, 0)),
            pl.BlockSpec((rows, N_DT), lambda i, g: (i, 0)),
            pl.BlockSpec((None,) + sel_both.shape[1:], lambda i, g: (g, 0, 0)),
            pl.BlockSpec((SCAN_CHUNKS, D_STATE, GROUP_WIDTH), lambda i, g: (i, 0, g)),
            pl.BlockSpec((SCAN_CHUNKS, D_STATE, GROUP_WIDTH), lambda i, g: (i, 0, g)),
            pl.BlockSpec((rows, GROUP_WIDTH), tile),
            pl.BlockSpec((1, GROUP_WIDTH), lambda i, g: (0, g)),
            pl.BlockSpec((1, GROUP_WIDTH), lambda i, g: (0, g)),
        ],
        out_specs=pl.BlockSpec((rows, GROUP_WIDTH), tile),
        out_shape=jax.ShapeDtypeStruct((t, D_INNER), BF16),
        scratch_shapes=[pltpu.VMEM((SCAN_CHUNKS, N_DT, CHUNK), F32)],
        compiler_params=_cparams(("parallel", "arbitrary")),
        name="ssd_out",
    )(xbc, xbc, xbc, cs, dt, sel_both, prev_f, prev_b, zxbc, d_skip, norm_g)


def _rope_tables(seq_len):
    rows = seq_len // GRID_W
    row = jnp.repeat(jnp.arange(rows, dtype=F32), GRID_W)
    col = jnp.tile(jnp.arange(GRID_W, dtype=F32), rows)
    inv_freq = ROPE_THETA ** (-jnp.arange(0, ROPE_AXIS_DIM, 2, dtype=F32) / ROPE_AXIS_DIM)
    ang = jnp.stack([row, col], axis=-1)[..., None] * inv_freq
    cos = jnp.cos(ang).reshape(seq_len, ROPE_AXIS_DIM)
    sin = jnp.sin(ang).reshape(seq_len, ROPE_AXIS_DIM)
    rc = jnp.concatenate([cos, cos], axis=1)
    rs = jnp.concatenate([-sin, sin], axis=1)
    return rc, rs


def _permute_rope_heads(a):
    lead = a.shape[:-1]
    a = a.reshape(lead + (a.shape[-1] // HEAD_DIM, 2, 2, ROPE_AXIS_DIM // 2))
    return jnp.swapaxes(a, -2, -3).reshape(lead + (-1,))


def _head_selectors():
    col = jnp.arange(N_DT)[None, :, None]
    dg = jnp.arange(2 * SSD_GROUPS)[:, None, None]
    first = (dg // SSD_GROUPS) * SSD_HEADS + (dg % SSD_GROUPS) * HEADS_PER_GROUP
    lane_head = jnp.arange(GROUP_WIDTH)[None, None, :] // SSD_HEAD_DIM
    e = (col == first + lane_head).astype(BF16)
    return jnp.concatenate([e, e, e], axis=1)


def kernel(x_prompt, x_sample, norm_mix, norm_mlp, attn_w_qkv, attn_q_norm, attn_k_norm, attn_w_o,
           ssd_w_in, ssd_conv_w, ssd_conv_b, ssd_dt_bias, ssd_a_log, ssd_d, ssd_norm, ssd_w_out,
           mlp_w_up, mlp_w_down):
    seq_len = x_prompt.shape[1]
    assert x_sample.shape[1] == seq_len and x_prompt.shape[2] == D_MODEL
    n_prompt = x_prompt.shape[0] * seq_len
    h = jnp.concatenate([x_prompt.reshape(-1, D_MODEL), x_sample.reshape(-1, D_MODEL)], axis=0)
    depth = norm_mix.shape[0]

    rc, rs = _rope_tables(seq_len)
    sel = _head_selectors()
    sel_both = jnp.concatenate([sel[:SSD_GROUPS], sel[SSD_GROUPS:]], axis=-1)

    for i in range(depth):
        jdx = i // 2
        g_mix = norm_mix[i].reshape(1, D_MODEL)
        if i % 2 == 0:
            gains = _permute_rope_heads(jnp.stack([attn_q_norm[jdx] * QK_LOG2_SCALE, attn_k_norm[jdx]], axis=0))
            w_qkv = attn_w_qkv[jdx].astype(BF16)
            w_qkv = jnp.concatenate([_permute_rope_heads(w_qkv[:, :D_Q + D_KV]), w_qkv[:, D_Q + D_KV:]], axis=1)
            qkv = _qkv_proj(h, g_mix, w_qkv, gains, rc, rs, seq_len)
            o = _flash_attention(qkv, attn_q_norm[jdx], attn_k_norm[jdx], seq_len)
            h = _proj_residual(o, attn_w_o[jdx].astype(BF16), h)
        else:
            w_in = ssd_w_in[jdx]
            zxbc, dt, cs = _in_proj(h, g_mix, w_in[:, :D_ZXBC].astype(BF16),
                                    w_in[:, D_ZXBC:].astype(BF16),
                                    ssd_dt_bias[jdx].reshape(1, N_DT), ssd_a_log[jdx].reshape(1, N_DT))
            xbc = _conv_silu(zxbc, ssd_conv_w[jdx], ssd_conv_b[jdx].reshape(1, CONV_DIM), seq_len)
            prev_f, prev_b = _ssd_states(xbc, dt, cs, sel, seq_len)
            d_skip = jnp.repeat(ssd_d[jdx], SSD_HEAD_DIM).reshape(1, D_INNER)
            y = _ssd_out(xbc, zxbc, dt, cs, sel_both, prev_f, prev_b, d_skip,
                         ssd_norm[jdx].reshape(1, D_INNER))
            h = _proj_residual(y, ssd_w_out[jdx].astype(BF16), h)
        mlp_args = (norm_mlp[i].reshape(1, D_MODEL), mlp_w_up[i].astype(BF16), mlp_w_down[i].astype(BF16))
        if i + 1 < depth:
            h = _mlp(h, *mlp_args)
        else:
            y_prompt = _mlp(h, *mlp_args, row_start=0, n_rows=n_prompt)
            y_sample = _mlp(h, *mlp_args, row_start=n_prompt, n_rows=h.shape[0] - n_prompt)
    return (y_prompt.reshape(x_prompt.shape), y_sample.reshape(x_sample.shape))
```
